```python
import jax, jax.numpy as jnp
from jax import lax
import numpy as np

D_MODEL = 1024
BATCH = 4
SEQ = 8192
DEPTH = 1
DEC_BATCH = 128
DEC_SEQ = 1
PAST_LEN = 8192
PAGE_SIZE = 128

N_HEADS = 8
HEAD_DIM = D_MODEL // 2 // N_HEADS
ATT_WIDTH = N_HEADS * HEAD_DIM
N_IDX_HEADS = 8
IDX_HEAD_DIM = 64
TOPK_MAX = 256
Q_BLOCK = 128
POOL_WIDTH = D_MODEL - ATT_WIDTH
POOL_WINDOWS = (2, 4, 8, 16)
N_POOL_GROUPS = len(POOL_WINDOWS)
POOL_GROUP_DIM = POOL_WIDTH // N_POOL_GROUPS
POOL_STATE_LEN = max(POOL_WINDOWS) - 1
N_MEM = 256
N_MEM_HEADS = 4
MEM_HEAD_DIM = 128
MEM_WIDTH = N_MEM_HEADS * MEM_HEAD_DIM
PEER_HEADS = 8
PEER_N_KEYS = 128
PEER_N_EXPERTS = PEER_N_KEYS * PEER_N_KEYS
PEER_QUERY_DIM = 256
PEER_HALF_DIM = PEER_QUERY_DIM // 2
PEER_TOPK = 16
PEER_BLOCK = 256
ROPE_THETA = 10000.0
RMS_EPS = 1e-6
IN_SPLITS = (ATT_WIDTH, ATT_WIDTH, ATT_WIDTH, N_IDX_HEADS * IDX_HEAD_DIM, IDX_HEAD_DIM, N_IDX_HEADS, POOL_WIDTH)
IN_WIDTH = sum(IN_SPLITS)

kernel_name = 'hymba_dsa_pool_peer_step'

F32 = jnp.float32


def rmsnorm(x, g):
    xf = x.astype(F32)
    y = xf * lax.rsqrt(jnp.mean(xf * xf, axis=-1, keepdims=True) + RMS_EPS)
    return (y * g.astype(F32)).astype(x.dtype)


def rope(x, pos):
    half = x.shape[-1] // 2
    inv_freq = ROPE_THETA ** (-jnp.arange(half, dtype=F32) / half)
    ang = pos.astype(F32)[:, None] * inv_freq[None, :]
    cos = jnp.cos(ang)[None, :, None, :]
    sin = jnp.sin(ang)[None, :, None, :]
    xf = x.astype(F32)
    x1, x2 = xf[..., :half], xf[..., half:]
    return jnp.concatenate([x1 * cos - x2 * sin, x2 * cos + x1 * sin], axis=-1).astype(x.dtype)


def mixer_projections(x, norm_mix, w_in, first_pos):
    B, T, _ = x.shape
    h = rmsnorm(x, norm_mix)
    z = h @ w_in
    q, k, v, qi, ki, wi, p = jnp.split(z, list(np.cumsum(IN_SPLITS)[:-1]), axis=-1)
    pos = first_pos + jnp.arange(T, dtype=jnp.int32)
    q = rope(q.reshape(B, T, N_HEADS, HEAD_DIM), pos)
    k = rope(k.reshape(B, T, N_HEADS, HEAD_DIM), pos)
    v = v.reshape(B, T, N_HEADS, HEAD_DIM)
    qi = rope(qi.reshape(B, T, N_IDX_HEADS, IDX_HEAD_DIM), pos)
    ki = rope(ki[:, :, None, :], pos)[:, :, 0, :]
    wi = wi * (N_IDX_HEADS ** -0.5)
    return q, k, v, qi, ki, wi, p


def index_scores(qi, wi, ki):
    dots = jnp.einsum('bthd,bsd->bths', qi.astype(F32), ki.astype(F32))
    return jnp.einsum('bths,bth->bts', jax.nn.relu(dots) * (IDX_HEAD_DIM ** -0.5), wi.astype(F32))


def attend_selected(q, k_sel, v_sel, valid):
    s = jnp.einsum('bthd,btjhd->bthj', q.astype(F32), k_sel.astype(F32)) * (HEAD_DIM ** -0.5)
    s = jnp.where(valid[:, :, None, :], s, -jnp.inf)
    p = jax.nn.softmax(s, axis=-1)
    return jnp.einsum('bthj,btjhd->bthd', p, v_sel.astype(F32)).astype(q.dtype)


def dsa_prompt(q, k, v, qi, ki, wi):
    B, T, H, Dh = q.shape
    k_top = min(TOPK_MAX, T // 4)
    nb = T // Q_BLOCK

    def to_blocks(a):
        return jnp.moveaxis(a.reshape((B, nb, Q_BLOCK) + a.shape[2:]), 1, 0)

    starts = jnp.arange(nb, dtype=jnp.int32) * Q_BLOCK
    key_pos = jnp.arange(T, dtype=jnp.int32)

    def block(args):
        qb, qib, wib, t0 = args
        tq = t0 + jnp.arange(Q_BLOCK, dtype=jnp.int32)
        sc = index_scores(qib, wib, ki)
        sc = jnp.where((key_pos[None, :] <= tq[:, None])[None], sc, -jnp.inf)
        _, idx = lax.top_k(sc, k_top)
        valid = idx <= tq[None, :, None]
        k_sel = jax.vmap(lambda kk, ii: kk[ii])(k, idx)
        v_sel = jax.vmap(lambda vv, ii: vv[ii])(v, idx)
        return attend_selected(qb, k_sel, v_sel, valid)

    out = lax.map(block, (to_blocks(q), to_blocks(qi), to_blocks(wi), starts))
    return jnp.moveaxis(out, 0, 1).reshape(B, T, H * Dh)


def dsa_sample(q, k_new, v_new, qi, ki_new, wi, cache_k, cache_v, cache_idx_k, page_table, layer):
    DB, DS, H, Dh = q.shape
    past = page_table.shape[1] * PAGE_SIZE
    L = past + DS
    k_top = min(TOPK_MAX, L // 4)
    ki_past = cache_idx_k[layer, page_table].reshape(DB, past, IDX_HEAD_DIM)
    ki_all = jnp.concatenate([ki_past, ki_new.astype(ki_past.dtype)], axis=1)
    tq = past + jnp.arange(DS, dtype=jnp.int32)
    sc = index_scores(qi, wi, ki_all)
    sc = jnp.where((jnp.arange(L, dtype=jnp.int32)[None, :] <= tq[:, None])[None], sc, -jnp.inf)
    _, idx = lax.top_k(sc, k_top)
    valid = idx <= tq[None, :, None]
    in_past = idx < past
    ip = jnp.minimum(idx, past - 1)
    b_ix = jnp.arange(DB, dtype=jnp.int32)[:, None, None]
    phys = page_table[b_ix, ip // PAGE_SIZE]
    off = ip % PAGE_SIZE
    inew = jnp.clip(idx - past, 0, DS - 1)
    k_sel = jnp.where(in_past[..., None, None], cache_k[layer, phys, off], k_new[b_ix, inew].astype(cache_k.dtype))
    v_sel = jnp.where(in_past[..., None, None], cache_v[layer, phys, off], v_new[b_ix, inew].astype(cache_v.dtype))
    return attend_selected(q, k_sel, v_sel, valid).reshape(DB, DS, H * Dh)


def pool_mixer(p, prefix, first_pos, pool_w, pool_scale):
    B, T, C = p.shape
    P = prefix.shape[1]
    ext = jnp.concatenate([prefix.astype(p.dtype), p], axis=1)
    extf = ext.astype(F32)
    cs = jnp.concatenate([jnp.zeros((B, 1, C), F32), jnp.cumsum(extf, axis=1)], axis=1)
    j = P + np.arange(T)
    abs_pos = first_pos + np.arange(T)
    outs = []
    for g, w in enumerate(POOL_WINDOWS):
        sl = slice(g * POOL_GROUP_DIM, (g + 1) * POOL_GROUP_DIM)
        cnt = np.minimum(w, abs_pos + 1)
        lo = j + 1 - cnt
        csg = cs[:, :, sl]
        mean = (csg[:, j + 1] - csg[:, lo]) / jnp.asarray(cnt.astype(np.float32))[None, :, None]
        outs.append(mean - extf[:, j, sl])
    pooled = jnp.stack(outs, axis=2)
    mixed = jnp.einsum('btgc,gcd->btgd', pooled, pool_w.astype(F32)).reshape(B, T, C)
    mixed = mixed * pool_scale.astype(F32)
    return mixed.astype(p.dtype), ext[:, -POOL_STATE_LEN:]


def memory_kv(mem, norm_mem, w_mk, w_mv):
    B, M, _ = mem.shape
    m = rmsnorm(mem, norm_mem)
    return ((m @ w_mk).reshape(B, M, N_MEM_HEADS, MEM_HEAD_DIM),
            (m @ w_mv).reshape(B, M, N_MEM_HEADS, MEM_HEAD_DIM))


def cross_attend(h, mem_k, mem_v, w_mq, w_mo):
    B, T, _ = h.shape
    q = (h @ w_mq).reshape(B, T, N_MEM_HEADS, MEM_HEAD_DIM)
    s = jnp.einsum('bthd,bmhd->bthm', q.astype(F32), mem_k.astype(F32)) * (MEM_HEAD_DIM ** -0.5)
    p = jax.nn.softmax(s, axis=-1)
    o = jnp.einsum('bthm,bmhd->bthd', p, mem_v.astype(F32)).astype(h.dtype)
    return o.reshape(B, T, MEM_WIDTH) @ w_mo


def peer_ffn(h, w_pq, sub_keys, peer_u, peer_v):
    B, T, D = h.shape
    n = B * T
    nblk = -(-n // PEER_BLOCK)
    xs = jnp.pad(h.reshape(n, D), ((0, nblk * PEER_BLOCK - n), (0, 0))).reshape(nblk, PEER_BLOCK, D)
    n_sel = PEER_HEADS * PEER_TOPK

    def block(xb):
        q = (xb @ w_pq).astype(F32).reshape(PEER_BLOCK, PEER_HEADS, 2, PEER_HALF_DIM)
        s = jnp.einsum('nhpc,hpkc->nhpk', q, sub_keys.astype(F32))
        s1, i1 = lax.top_k(s[:, :, 0], PEER_TOPK)
        s2, i2 = lax.top_k(s[:, :, 1], PEER_TOPK)
        cand_s = (s1[..., :, None] + s2[..., None, :]).reshape(PEER_BLOCK, PEER_HEADS, PEER_TOPK * PEER_TOPK)
        cand_id = (i1[..., :, None] * PEER_N_KEYS + i2[..., None, :]).reshape(PEER_BLOCK, PEER_HEADS, PEER_TOPK * PEER_TOPK)
        top_s, pos = lax.top_k(cand_s, PEER_TOPK)
        eid = jnp.take_along_axis(cand_id, pos, axis=-1).reshape(PEER_BLOCK, n_sel)
        gate = jax.nn.softmax(top_s, axis=-1).reshape(PEER_BLOCK, n_sel)
        act = jax.nn.gelu(jnp.einsum('nkd,nd->nk', peer_u[eid], xb, preferred_element_type=F32), approximate=False)
        return jnp.einsum('nk,nkd->nd', (gate * act).astype(xb.dtype), peer_v[eid])

    out = lax.map(block, xs)
    return out.reshape(nblk * PEER_BLOCK, D)[:n].reshape(B, T, D)


def layer_tail(x, attn_out, pool_out, mem_k, mem_v, w_out, norm_cross, w_mq, w_mo,
               norm_ffn, w_pq, sub_keys, peer_u, peer_v):
    x = x + jnp.concatenate([attn_out, pool_out], axis=-1) @ w_out
    x = x + cross_attend(rmsnorm(x, norm_cross), mem_k, mem_v, w_mq, w_mo)
    x = x + peer_ffn(rmsnorm(x, norm_ffn), w_pq, sub_keys, peer_u, peer_v)
    return x


def setup_inputs(seed: int = 0) -> dict:
    key = jax.random.key(seed)
    ks = jax.random.split(key, 32)
    n_pages = PAST_LEN // PAGE_SIZE
    n_used = DEC_BATCH * n_pages
    n_pool = n_used + max(1, n_used // 4)

    def nrm(k, shape, scale=1.0):
        return jax.random.normal(k, shape, F32) * scale

    def gain(k):
        return 1.0 + nrm(k, (DEPTH, D_MODEL), 0.02)

    page_table = jax.random.permutation(ks[9], n_pool)[:n_used].reshape(DEC_BATCH, n_pages).astype(jnp.int32)
    return {
        'x_prompt': nrm(ks[0], (BATCH, SEQ, D_MODEL)),
        'x_sample': nrm(ks[1], (DEC_BATCH, DEC_SEQ, D_MODEL)),
        'mem_prompt': nrm(ks[2], (BATCH, N_MEM, D_MODEL)),
        'cache_k': nrm(ks[3], (DEPTH, n_pool, PAGE_SIZE, N_HEADS, HEAD_DIM)),
        'cache_v': nrm(ks[4], (DEPTH, n_pool, PAGE_SIZE, N_HEADS, HEAD_DIM)),
        'cache_idx_k': nrm(ks[5], (DEPTH, n_pool, PAGE_SIZE, IDX_HEAD_DIM)),
        'state_pool': nrm(ks[6], (DEPTH, DEC_BATCH, POOL_STATE_LEN, POOL_WIDTH)),
        'cache_mem_k': nrm(ks[7], (DEPTH, DEC_BATCH, N_MEM, N_MEM_HEADS, MEM_HEAD_DIM)),
        'cache_mem_v': nrm(ks[8], (DEPTH, DEC_BATCH, N_MEM, N_MEM_HEADS, MEM_HEAD_DIM)),
        'page_table': page_table,
        'norm_mix': gain(ks[10]),
        'w_in': nrm(ks[11], (DEPTH, D_MODEL, IN_WIDTH), D_MODEL ** -0.5),
        'pool_w': nrm(ks[12], (DEPTH, N_POOL_GROUPS, POOL_GROUP_DIM, POOL_GROUP_DIM), POOL_GROUP_DIM ** -0.5),
        'pool_scale': 1.0 + nrm(ks[13], (DEPTH, POOL_WIDTH), 0.02),
        'w_out': nrm(ks[14], (DEPTH, ATT_WIDTH + POOL_WIDTH, D_MODEL), (ATT_WIDTH + POOL_WIDTH) ** -0.5),
        'norm_cross': gain(ks[15]),
        'norm_mem': gain(ks[16]),
        'w_mq': nrm(ks[17], (DEPTH, D_MODEL, MEM_WIDTH), D_MODEL ** -0.5),
        'w_mk': nrm(ks[18], (DEPTH, D_MODEL, MEM_WIDTH), D_MODEL ** -0.5),
        'w_mv': nrm(ks[19], (DEPTH, D_MODEL, MEM_WIDTH), D_MODEL ** -0.5),
        'w_mo': nrm(ks[20], (DEPTH, MEM_WIDTH, D_MODEL), MEM_WIDTH ** -0.5),
        'norm_ffn': gain(ks[21]),
        'w_pq': nrm(ks[22], (DEPTH, D_MODEL, PEER_HEADS * PEER_QUERY_DIM), D_MODEL ** -0.5),
        'sub_keys': nrm(ks[23], (DEPTH, PEER_HEADS, 2, PEER_N_KEYS, PEER_HALF_DIM), PEER_HALF_DIM ** -0.5),
        'peer_u': nrm(ks[24], (DEPTH, PEER_N_EXPERTS, D_MODEL), D_MODEL ** -0.5),
        'peer_v': nrm(ks[25], (DEPTH, PEER_N_EXPERTS, D_MODEL), PEER_HEADS ** -0.5),
        'norm_final': 1.0 + nrm(ks[26], (D_MODEL,), 0.02),
    }


def reference(x_prompt, x_sample, mem_prompt, cache_k, cache_v, cache_idx_k, state_pool,
              cache_mem_k, cache_mem_v, page_table, norm_mix, w_in, pool_w, pool_scale, w_out,
              norm_cross, norm_mem, w_mq, w_mk, w_mv, w_mo, norm_ffn, w_pq, sub_keys, peer_u,
              peer_v, norm_final):
    yp, ys = x_prompt, x_sample
    kp_l, vp_l, kip_l, pp_l, mkp_l, mvp_l = [], [], [], [], [], []
    ks_l, vs_l, kis_l, ps_l = [], [], [], []
    for l in range(DEPTH):
        tail_w = (w_out[l], norm_cross[l], w_mq[l], w_mo[l], norm_ffn[l], w_pq[l], sub_keys[l], peer_u[l], peer_v[l])
        q, k, v, qi, ki, wi, p = mixer_projections(yp, norm_mix[l], w_in[l], 0)
        att = dsa_prompt(q, k, v, qi, ki, wi)
        pool_out, pst = pool_mixer(p, p[:, :0], 0, pool_w[l], pool_scale[l])
        mk, mv = memory_kv(mem_prompt, norm_mem[l], w_mk[l], w_mv[l])
        yp = layer_tail(yp, att, pool_out, mk, mv, *tail_w)
        kp_l.append(k); vp_l.append(v); kip_l.append(ki); pp_l.append(pst); mkp_l.append(mk); mvp_l.append(mv)
        q, k, v, qi, ki, wi, p = mixer_projections(ys, norm_mix[l], w_in[l], PAST_LEN)
        att = dsa_sample(q, k, v, qi, ki, wi, cache_k, cache_v, cache_idx_k, page_table, l)
        pool_out, pst = pool_mixer(p, state_pool[l], PAST_LEN, pool_w[l], pool_scale[l])
        ys = layer_tail(ys, att, pool_out, cache_mem_k[l], cache_mem_v[l], *tail_w)
        ks_l.append(k); vs_l.append(v); kis_l.append(ki); ps_l.append(pst)
    y_prompt = rmsnorm(yp, norm_final)
    y_sample = rmsnorm(ys, norm_final)
    new_k_prompt = jnp.stack(kp_l, 0)
    new_v_prompt = jnp.stack(vp_l, 0)
    new_idx_k_prompt = jnp.stack(kip_l, 0)
    new_pool_prompt = jnp.stack(pp_l, 0)
    new_mem_k_prompt = jnp.stack(mkp_l, 0)
    new_mem_v_prompt = jnp.stack(mvp_l, 0)
    new_k_sample = jnp.stack(ks_l, 0)
    new_v_sample = jnp.stack(vs_l, 0)
    new_idx_k_sample = jnp.stack(kis_l, 0)
    new_pool_sample = jnp.stack(ps_l, 0)
    return (y_prompt, y_sample, new_k_prompt, new_v_prompt, new_idx_k_prompt, new_pool_prompt,
            new_mem_k_prompt, new_mem_v_prompt, new_k_sample, new_v_sample, new_idx_k_sample, new_pool_sample)
```

```python
import functools

import numpy as np
import jax
import jax.numpy as jnp
from jax import lax
from jax.experimental import pallas as pl
from jax.experimental.pallas import tpu as pltpu
from jax.experimental.pallas import tpu_sc as plsc

F32 = jnp.float32
BF16 = jnp.bfloat16
I32 = jnp.int32

N_HEADS = 8
HEAD_DIM = 64
ATT_WIDTH = N_HEADS * HEAD_DIM
N_IDX_HEADS = 8
IDX_HEAD_DIM = 64
TOPK_MAX = 256
POOL_WINDOWS = (2, 4, 8, 16)
POOL_GROUP_DIM = 128
POOL_WIDTH = POOL_GROUP_DIM * len(POOL_WINDOWS)
POOL_STATE_LEN = max(POOL_WINDOWS) - 1
N_MEM_HEADS = 4
MEM_HEAD_DIM = 128
MEM_WIDTH = N_MEM_HEADS * MEM_HEAD_DIM
PEER_HEADS = 8
PEER_N_KEYS = 128
PEER_HALF_DIM = 128
PEER_TOPK = 16
N_SEL = PEER_HEADS * PEER_TOPK
PAGE_SIZE = 128
ROPE_THETA = 10000.0
RMS_EPS = 1e-6

LANES = 128
MASK_VALUE = -1e30
INT_MIN = np.int32(-2 ** 31)
NEG_INF_KEY = np.int32(np.uint32(0x807FFFFF).astype(np.int64) - 2 ** 32)

VMEM_LIMIT = 56 * 1024 * 1024


def _cparams(*sem):
    return pltpu.CompilerParams(dimension_semantics=sem, vmem_limit_bytes=VMEM_LIMIT)


def _rms(x, g):
    ms = jnp.mean(x * x, axis=-1, keepdims=True)
    return x * lax.rsqrt(ms + RMS_EPS) * g


def _inproj_kernel(x_ref, g_ref, wbig_ref, wsm_ref, cos_ref, sin_ref,
                   q_ref, kf_ref, kb_ref, vf_ref, vb_ref, qi_ref, kif_ref, kib_ref, wi_ref, p_ref):
    h = _rms(x_ref[...], g_ref[...]).astype(BF16)
    cos = cos_ref[...]
    sin = sin_ref[...]
    lane = lax.broadcasted_iota(I32, (1, LANES), 1)
    first_half = (lane % HEAD_DIM) < (HEAD_DIM // 2)

    def rope(z):
        partner = jnp.where(first_half, pltpu.roll(z, LANES - HEAD_DIM // 2, 1),
                            pltpu.roll(z, HEAD_DIM // 2, 1))
        return z * cos + partner * sin

    def proj(c0, width):
        return jnp.dot(h, wbig_ref[:, c0:c0 + width], preferred_element_type=F32)

    for s in range(ATT_WIDTH // LANES):
        sl = slice(s * LANES, (s + 1) * LANES)
        zq = rope(proj(s * LANES, LANES))
        q_ref[:, sl] = (zq * (HEAD_DIM ** -0.5)).astype(BF16)
        zk = rope(proj(ATT_WIDTH + s * LANES, LANES))
        kf_ref[:, sl] = zk
        kb_ref[:, sl] = zk.astype(BF16)
        zqi = rope(proj(3 * ATT_WIDTH + s * LANES, LANES)).astype(BF16)
        qi_ref[2 * s] = zqi[:, :IDX_HEAD_DIM]
        qi_ref[2 * s + 1] = zqi[:, IDX_HEAD_DIM:]
    zv = proj(2 * ATT_WIDTH, ATT_WIDTH)
    vf_ref[...] = zv
    vb_ref[...] = zv.astype(BF16)
    p_ref[...] = proj(4 * ATT_WIDTH, POOL_WIDTH)
    zs = jnp.dot(h, wsm_ref[...], preferred_element_type=F32)
    zki = rope(zs)[:, :IDX_HEAD_DIM]
    kif_ref[...] = zki
    kib_ref[...] = zki.astype(BF16)
    wi_ref[...] = zs[:, IDX_HEAD_DIM:IDX_HEAD_DIM + N_IDX_HEADS] * (
        (N_IDX_HEADS ** -0.5) * (IDX_HEAD_DIM ** -0.5))


def _inproj(x, g, wbig, wsm, cos, sin, tm):
    n, d = x.shape
    nt = n // tm
    nrt = cos.shape[0] // tm
    row = lambda i: (i, 0)
    full = lambda i: (0, 0)
    out_shape = (
        jax.ShapeDtypeStruct((n, ATT_WIDTH), BF16),
        jax.ShapeDtypeStruct((n, ATT_WIDTH), F32),
        jax.ShapeDtypeStruct((n, ATT_WIDTH), BF16),
        jax.ShapeDtypeStruct((n, ATT_WIDTH), F32),
        jax.ShapeDtypeStruct((n, ATT_WIDTH), BF16),
        jax.ShapeDtypeStruct((N_IDX_HEADS, n, IDX_HEAD_DIM), BF16),
        jax.ShapeDtypeStruct((n, IDX_HEAD_DIM), F32),
        jax.ShapeDtypeStruct((n, IDX_HEAD_DIM), BF16),
        jax.ShapeDtypeStruct((n, N_IDX_HEADS), F32),
        jax.ShapeDtypeStruct((n, POOL_WIDTH), F32),
    )
    out_specs = (
        pl.BlockSpec((tm, ATT_WIDTH), row), pl.BlockSpec((tm, ATT_WIDTH), row),
        pl.BlockSpec((tm, ATT_WIDTH), row), pl.BlockSpec((tm, ATT_WIDTH), row),
        pl.BlockSpec((tm, ATT_WIDTH), row),
        pl.BlockSpec((N_IDX_HEADS, tm, IDX_HEAD_DIM), lambda i: (0, i, 0)),
        pl.BlockSpec((tm, IDX_HEAD_DIM), row), pl.BlockSpec((tm, IDX_HEAD_DIM), row),
        pl.BlockSpec((tm, N_IDX_HEADS), row), pl.BlockSpec((tm, POOL_WIDTH), row),
    )
    return pl.pallas_call(
        _inproj_kernel, grid=(nt,),
        in_specs=[pl.BlockSpec((tm, d), row), pl.BlockSpec((1, d), full),
                  pl.BlockSpec(wbig.shape, full), pl.BlockSpec(wsm.shape, full),
                  pl.BlockSpec((tm, LANES), lambda i: (i % nrt, 0)),
                  pl.BlockSpec((tm, LANES), lambda i: (i % nrt, 0))],
        out_specs=out_specs, out_shape=out_shape,
        compiler_params=_cparams("parallel"), name="inproj",
    )(x, g, wbig, wsm, cos, sin)


HALO = 16


def _pool_prompt_kernel(p_ref, pw_ref, ps_ref, o_ref, ext_ref):
    tm = p_ref.shape[0]
    i = pl.program_id(1)

    @pl.when(i == 0)
    def _():
        ext_ref[0:HALO, :] = jnp.zeros((HALO, POOL_WIDTH), F32)

    @pl.when(i > 0)
    def _():
        ext_ref[0:HALO, :] = ext_ref[tm:tm + HALO, :]

    ext_ref[HALO:HALO + tm, :] = p_ref[...]
    pos = i * tm + lax.broadcasted_iota(I32, (tm, 1), 0)
    for g, w in enumerate(POOL_WINDOWS):
        sl = slice(g * POOL_GROUP_DIM, (g + 1) * POOL_GROUP_DIM)
        cur = ext_ref[HALO:HALO + tm, sl]
        s = cur
        for j in range(1, w):
            s = s + ext_ref[HALO - j:HALO - j + tm, sl]
        cnt = jnp.minimum(w, pos + 1).astype(F32)
        pooled = s / cnt - cur
        mixed = jnp.dot(pooled.astype(BF16), pw_ref[g], preferred_element_type=F32)
        o_ref[:, sl] = (mixed * ps_ref[:, sl]).astype(BF16)


def _pool_prompt(p, pw, ps, batch, tm):
    n = p.shape[0]
    nt = n // batch // tm
    return pl.pallas_call(
        _pool_prompt_kernel, grid=(batch, nt),
        in_specs=[pl.BlockSpec((tm, POOL_WIDTH), lambda b, i: (b * nt + i, 0)),
                  pl.BlockSpec(pw.shape, lambda b, i: (0, 0, 0)),
                  pl.BlockSpec((1, POOL_WIDTH), lambda b, i: (0, 0))],
        out_specs=pl.BlockSpec((tm, POOL_WIDTH), lambda b, i: (b * nt + i, 0)),
        out_shape=jax.ShapeDtypeStruct((n, POOL_WIDTH), BF16),
        scratch_shapes=[pltpu.VMEM((HALO + tm, POOL_WIDTH), F32)],
        compiler_params=_cparams("arbitrary", "arbitrary"), name="pool_prompt",
    )(p, pw, ps)


def _pool_sample_kernel(st_ref, p_ref, pw_ref, ps_ref, o_ref):
    for g, w in enumerate(POOL_WINDOWS):
        sl = slice(g * POOL_GROUP_DIM, (g + 1) * POOL_GROUP_DIM)
        cur = p_ref[:, sl]
        s = cur
        for j in range(1, w):
            s = s + st_ref[POOL_STATE_LEN - j, :, sl]
        pooled = s / float(w) - cur
        mixed = jnp.dot(pooled.astype(BF16), pw_ref[g], preferred_element_type=F32)
        o_ref[:, sl] = (mixed * ps_ref[:, sl]).astype(BF16)


def _pool_sample(state_t, p, pw, ps):
    n = p.shape[0]
    return pl.pallas_call(
        _pool_sample_kernel,
        out_shape=jax.ShapeDtypeStruct((n, POOL_WIDTH), BF16),
        compiler_params=pltpu.CompilerParams(vmem_limit_bytes=VMEM_LIMIT), name="pool_sample",
    )(state_t, p, pw, ps)


def _sort_key(x):
    b = lax.bitcast_convert_type(x, I32)
    return b ^ ((b >> 31) & np.int32(0x7FFFFFFF))


def _count_ge(skey_ref, n_chunks, chunk, cand):
    rows = skey_ref.shape[0]
    candb = jnp.broadcast_to(cand, (rows, LANES))

    def body(c, acc):
        off = pl.multiple_of(c * chunk, chunk)
        blk = skey_ref[:, pl.ds(off, chunk)]
        for j in range(chunk // LANES):
            acc = acc + jnp.where(blk[:, j * LANES:(j + 1) * LANES] >= candb, 1.0, 0.0)
        return acc

    acc = lax.fori_loop(0, n_chunks, body, jnp.zeros((rows, LANES), F32))
    return jnp.sum(acc, axis=1, keepdims=True)


def _select_threshold(skey_ref, n_chunks, chunk, k_top):
    rows = skey_ref.shape[0]
    kf = float(k_top)

    def bit_step(i, prefix):
        cand_u = prefix | lax.shift_left(jnp.int32(1), jnp.asarray(31 - i, I32))
        cnt = _count_ge(skey_ref, n_chunks, chunk, cand_u ^ INT_MIN)
        return jnp.where(cnt >= kf, cand_u, prefix)

    prefix = lax.fori_loop(0, 32, bit_step, jnp.zeros((rows, 1), I32))
    tau = prefix ^ INT_MIN
    n_ge = _count_ge(skey_ref, n_chunks, chunk, tau)
    tie_row = (n_ge > kf) & (tau > NEG_INF_KEY)
    any_tie = jnp.max(jnp.where(tie_row, 1.0, 0.0)) > 0.0

    @pl.when(any_tie)
    def _():
        n_gt = _count_ge(skey_ref, n_chunks, chunk, tau + 1)
        need = kf - n_gt
        r = lax.broadcasted_iota(I32, (chunk, chunk), 0)
        c = lax.broadcasted_iota(I32, (chunk, chunk), 1)
        before = jnp.where(r < c, 1.0, 0.0).astype(BF16)

        def body(ci, seen):
            off = pl.multiple_of(ci * chunk, chunk)
            blk = skey_ref[:, pl.ds(off, chunk)]
            eq = blk == tau
            eqf = jnp.where(eq, 1.0, 0.0)
            rank = seen + jnp.dot(eqf.astype(BF16), before, preferred_element_type=F32)
            drop = eq & (rank >= need) & tie_row
            skey_ref[:, pl.ds(off, chunk)] = jnp.where(drop, tau - 1, blk)
            return seen + jnp.sum(eqf, axis=1, keepdims=True)

        lax.fori_loop(0, n_chunks, body, jnp.zeros((rows, 1), F32))

    return jnp.maximum(tau, NEG_INF_KEY + 1)


DSA_TQ = 128
DSA_TK = 512


def _dsa_prompt_kernel(k_top, q_ref, qi_ref, wi_ref, k_ref, v_ref, ki_ref, o_ref,
                       skey_ref, qm_ref, m_ref, l_ref, acc_ref):
    tq, tk = DSA_TQ, DSA_TK
    qb = pl.program_id(1)
    n_chunks = (qb * tq) // tk + 1
    t_row = qb * tq + lax.broadcasted_iota(I32, (tq, 1), 0)

    qi = qi_ref[...].reshape(N_IDX_HEADS * tq, IDX_HEAD_DIM)
    wi = wi_ref[...]

    def score_chunk(c, carry):
        off = pl.multiple_of(c * tk, tk)
        dots = lax.dot_general(qi, ki_ref[pl.ds(off, tk), :], (((1,), (1,)), ((), ())),
                               preferred_element_type=F32)
        sc = None
        for h in range(N_IDX_HEADS):
            term = jnp.maximum(dots[h * tq:(h + 1) * tq], 0.0) * wi[:, h:h + 1]
            sc = term if sc is None else sc + term
        key_pos = off + lax.broadcasted_iota(I32, (1, tk), 1)
        sc = jnp.where(key_pos <= t_row, sc, -jnp.inf)
        skey_ref[:, pl.ds(off, tk)] = _sort_key(sc)
        return carry

    lax.fori_loop(0, n_chunks, score_chunk, 0)

    thr = _select_threshold(skey_ref, n_chunks, tk, k_top)

    lane = lax.broadcasted_iota(I32, (1, LANES), 1)
    low = lane < HEAD_DIM
    n_pairs = N_HEADS // 2
    for p in range(n_pairs):
        slab = q_ref[:, p * LANES:(p + 1) * LANES]
        qm_ref[p, 0:tq, :] = jnp.where(low, slab, jnp.zeros_like(slab))
        qm_ref[p, tq:2 * tq, :] = jnp.where(low, jnp.zeros_like(slab), slab)
    m_ref[...] = jnp.full(m_ref.shape, MASK_VALUE, F32)
    l_ref[...] = jnp.zeros(l_ref.shape, F32)
    acc_ref[...] = jnp.zeros(acc_ref.shape, F32)

    def attn_chunk(c, carry):
        off = pl.multiple_of(c * tk, tk)
        bias = jnp.where(skey_ref[:, pl.ds(off, tk)] >= thr, 0.0, MASK_VALUE)
        for p in range(n_pairs):
            sl = slice(p * LANES, (p + 1) * LANES)
            s2 = lax.dot_general(qm_ref[p], k_ref[pl.ds(off, tk), sl], (((1,), (1,)), ((), ())),
                                 preferred_element_type=F32)
            probs = []
            for hh in range(2):
                rs = slice(hh * tq, (hh + 1) * tq)
                s = s2[rs] + bias
                m_old = m_ref[p, rs, :]
                m_new = jnp.maximum(m_old, jnp.max(s, axis=1, keepdims=True))
                alpha = jnp.exp(m_old - m_new)
                lsum = alpha * l_ref[p, rs, :]
                parts = []
                for j in range(tk // LANES):
                    e = jnp.exp(s[:, j * LANES:(j + 1) * LANES] - m_new)
                    lsum = lsum + e
                    parts.append(e.astype(BF16))
                m_ref[p, rs, :] = m_new
                l_ref[p, rs, :] = lsum
                acc_ref[p, rs, :] = alpha * acc_ref[p, rs, :]
                probs.append(jnp.concatenate(parts, axis=1))
            pv = jnp.dot(jnp.concatenate(probs, axis=0), v_ref[pl.ds(off, tk), sl],
                         preferred_element_type=F32)
            acc_ref[p] = acc_ref[p] + pv
        return carry

    lax.fori_loop(0, n_chunks, attn_chunk, 0)

    for p in range(n_pairs):
        outs = []
        for hh in range(2):
            rs = slice(hh * tq, (hh + 1) * tq)
            denom = jnp.sum(l_ref[p, rs, :], axis=1, keepdims=True)
            outs.append(acc_ref[p, rs, :] / denom)
        o_ref[:, p * LANES:(p + 1) * LANES] = jnp.where(low, outs[0], outs[1]).astype(BF16)


def _dsa_prompt(q, qi, wi, kb, vb, kib, batch, seq):
    n = q.shape[0]
    nq = seq // DSA_TQ
    k_top = min(TOPK_MAX, seq // 4)
    blk = lambda b, i: (b * nq + i, 0)
    per_batch = lambda b, i: (b, 0)
    n_pairs = N_HEADS // 2
    return pl.pallas_call(
        functools.partial(_dsa_prompt_kernel, k_top), grid=(batch, nq),
        in_specs=[pl.BlockSpec((DSA_TQ, ATT_WIDTH), blk),
                  pl.BlockSpec((N_IDX_HEADS, DSA_TQ, IDX_HEAD_DIM), lambda b, i: (0, b * nq + i, 0)),
                  pl.BlockSpec((DSA_TQ, N_IDX_HEADS), blk),
                  pl.BlockSpec((seq, ATT_WIDTH), per_batch),
                  pl.BlockSpec((seq, ATT_WIDTH), per_batch),
                  pl.BlockSpec((seq, IDX_HEAD_DIM), per_batch)],
        out_specs=pl.BlockSpec((DSA_TQ, ATT_WIDTH), blk),
        out_shape=jax.ShapeDtypeStruct((n, ATT_WIDTH), BF16),
        scratch_shapes=[pltpu.VMEM((DSA_TQ, seq), I32),
                        pltpu.VMEM((n_pairs, 2 * DSA_TQ, LANES), BF16),
                        pltpu.VMEM((n_pairs, 2 * DSA_TQ, LANES), F32),
                        pltpu.VMEM((n_pairs, 2 * DSA_TQ, LANES), F32),
                        pltpu.VMEM((n_pairs, 2 * DSA_TQ, LANES), F32)],
        compiler_params=_cparams("arbitrary", "arbitrary"), name="dsa_prompt",
    )(q, qi, wi, kb, vb, kib)


IDX_PAGES = 16
ATT_PAGES = 8


def _idx_sample_kernel(pt_ref, qi_ref, wi_ref, kin_ref, *refs):
    pages = refs[:IDX_PAGES]
    sc_ref, scn_ref = refs[IDX_PAGES], refs[IDX_PAGES + 1]
    qi = qi_ref[0]
    wi = wi_ref[0]
    nt = (((1,), (1,)), ((), ()))

    def score(keys):
        dots = lax.dot_general(qi, keys, nt, preferred_element_type=F32)
        return jnp.sum(jnp.maximum(dots, 0.0) * wi, axis=0, keepdims=True)

    for j in range(IDX_PAGES):
        sc_ref[0, :, j * PAGE_SIZE:(j + 1) * PAGE_SIZE] = score(pages[j][0].astype(BF16))

    @pl.when(pl.program_id(1) == 0)
    def _():
        new = score(jnp.broadcast_to(kin_ref[0], (8, IDX_HEAD_DIM)))
        lane = lax.broadcasted_iota(I32, (1, LANES), 1)
        scn_ref[0] = jnp.where(lane == 0, jnp.broadcast_to(new[:, 0:1], (1, LANES)), -jnp.inf)


def _idx_sample(page_table, qi_s, wi_s, ki_new, cache_idx_k):
    db, n_pages = page_table.shape
    nj = n_pages // IDX_PAGES
    past = n_pages * PAGE_SIZE

    def page_spec(j):
        return pl.BlockSpec((1, PAGE_SIZE, IDX_HEAD_DIM),
                            lambda b, s, pt, j=j: (pt[b, s * IDX_PAGES + j], 0, 0))

    grid_spec = pltpu.PrefetchScalarGridSpec(
        num_scalar_prefetch=1, grid=(db, nj),
        in_specs=[pl.BlockSpec((1, N_IDX_HEADS, IDX_HEAD_DIM), lambda b, s, pt: (b, 0, 0)),
                  pl.BlockSpec((1, N_IDX_HEADS, 1), lambda b, s, pt: (b, 0, 0)),
                  pl.BlockSpec((1, 1, IDX_HEAD_DIM), lambda b, s, pt: (b, 0, 0))]
                 + [page_spec(j) for j in range(IDX_PAGES)],
        out_specs=[pl.BlockSpec((1, 1, IDX_PAGES * PAGE_SIZE), lambda b, s, pt: (b, 0, s)),
                   pl.BlockSpec((1, 1, LANES), lambda b, s, pt: (b, 0, 0))])
    return pl.pallas_call(
        _idx_sample_kernel, grid_spec=grid_spec,
        out_shape=(jax.ShapeDtypeStruct((db, 1, past), F32),
                   jax.ShapeDtypeStruct((db, 1, LANES), F32)),
        compiler_params=_cparams("arbitrary", "arbitrary"), name="idx_sample",
    )(page_table, qi_s, wi_s, ki_new, *([cache_idx_k] * IDX_PAGES))


def _bias_sample_kernel(k_top, sc_ref, scn_ref, bias_ref, skey_ref):
    past = sc_ref.shape[1]
    skey_ref[:, 0:past] = _sort_key(sc_ref[...])
    skey_ref[:, past:past + LANES] = _sort_key(scn_ref[...])
    width = past + LANES
    thr = _select_threshold(skey_ref, width // LANES, LANES, k_top)
    bias_ref[...] = jnp.where(skey_ref[...] >= thr, 0.0, MASK_VALUE)


def _bias_sample(scores, score_new, k_top):
    db, past = scores.shape
    width = past + LANES
    return pl.pallas_call(
        functools.partial(_bias_sample_kernel, k_top),
        out_shape=jax.ShapeDtypeStruct((db, width), F32),
        scratch_shapes=[pltpu.VMEM((db, width), I32)],
        compiler_params=pltpu.CompilerParams(vmem_limit_bytes=VMEM_LIMIT), name="bias_sample",
    )(scores, score_new)


def _head_mask(n_heads, head_dim):
    width = n_heads * head_dim
    h = lax.broadcasted_iota(I32, (8, width), 0)
    l = lax.broadcasted_iota(I32, (8, width), 1)
    return (l // head_dim) == h


def _attn_sample_kernel(pt_ref, q_ref, kn_ref, vn_ref, bias_ref, biasn_ref, *refs):
    kp = refs[:ATT_PAGES]
    vp = refs[ATT_PAGES:2 * ATT_PAGES]
    o_ref = refs[2 * ATT_PAGES]
    m_ref, l_ref, acc_ref = refs[2 * ATT_PAGES + 1:]
    s_id = pl.program_id(1)
    hm = _head_mask(N_HEADS, HEAD_DIM)
    qbd = jnp.where(hm, jnp.broadcast_to(q_ref[0].astype(F32), hm.shape), 0.0).astype(BF16)
    nt = (((1,), (1,)), ((), ()))

    @pl.when(s_id == 0)
    def _():
        m_ref[...] = jnp.full(m_ref.shape, MASK_VALUE, F32)
        l_ref[...] = jnp.zeros(l_ref.shape, F32)
        acc_ref[...] = jnp.zeros(acc_ref.shape, F32)

    for j in range(ATT_PAGES):
        kpage = kp[j][0].astype(BF16)
        s = lax.dot_general(qbd, kpage, nt, preferred_element_type=F32)
        s = s + bias_ref[0, :, j * PAGE_SIZE:(j + 1) * PAGE_SIZE]
        m_old = m_ref[...]
        m_new = jnp.maximum(m_old, jnp.max(s, axis=1, keepdims=True))
        alpha = jnp.exp(m_old - m_new)
        e = jnp.exp(s - m_new)
        m_ref[...] = m_new
        l_ref[...] = alpha * l_ref[...] + e
        pv = jnp.dot(e.astype(BF16), vp[j][0].astype(BF16), preferred_element_type=F32)
        acc_ref[...] = alpha[:, 0:1] * acc_ref[...] + pv

    @pl.when(s_id == pl.num_programs(1) - 1)
    def _():
        kn = jnp.broadcast_to(kn_ref[0], hm.shape)
        s_new = jnp.sum(jnp.where(hm, qbd.astype(F32) * kn.astype(BF16).astype(F32), 0.0),
                        axis=1, keepdims=True) + biasn_ref[0, :, 0:1]
        m_old = m_ref[...]
        m_new = jnp.maximum(m_old, s_new)
        alpha = jnp.exp(m_old - m_new)
        e_new = jnp.exp(s_new - m_new)
        denom = jnp.sum(alpha * l_ref[...], axis=1, keepdims=True) + e_new[:, 0:1]
        vn = jnp.broadcast_to(vn_ref[0], hm.shape).astype(BF16).astype(F32)
        acc = alpha[:, 0:1] * acc_ref[...] + e_new[:, 0:1].astype(BF16).astype(F32) * vn
        out = jnp.where(hm, acc / denom, 0.0)
        o_ref[0] = jnp.sum(out, axis=0, keepdims=True).astype(BF16)


def _attn_sample(page_table, q_s, k_new, v_new, bias, cache_k, cache_v):
    db, n_pages = page_table.shape
    nj = n_pages // ATT_PAGES
    past = n_pages * PAGE_SIZE

    def page_spec(j):
        return pl.BlockSpec((1, PAGE_SIZE, ATT_WIDTH),
                            lambda b, s, pt, j=j: (pt[b, s * ATT_PAGES + j], 0, 0))

    tok = lambda b, s, pt: (b, 0, 0)
    grid_spec = pltpu.PrefetchScalarGridSpec(
        num_scalar_prefetch=1, grid=(db, nj),
        in_specs=[pl.BlockSpec((1, 1, ATT_WIDTH), tok), pl.BlockSpec((1, 1, ATT_WIDTH), tok),
                  pl.BlockSpec((1, 1, ATT_WIDTH), tok),
                  pl.BlockSpec((1, 1, ATT_PAGES * PAGE_SIZE), lambda b, s, pt: (b, 0, s)),
                  pl.BlockSpec((1, 1, LANES), lambda b, s, pt: (b, 0, past // LANES))]
                 + [page_spec(j) for j in range(ATT_PAGES)] * 2,
        out_specs=pl.BlockSpec((1, 1, ATT_WIDTH), tok),
        scratch_shapes=[pltpu.VMEM((8, LANES), F32), pltpu.VMEM((8, LANES), F32),
                        pltpu.VMEM((8, ATT_WIDTH), F32)])
    bias3 = bias.reshape(db, 1, past + LANES)
    return pl.pallas_call(
        _attn_sample_kernel, grid_spec=grid_spec,
        out_shape=jax.ShapeDtypeStruct((db, 1, ATT_WIDTH), BF16),
        compiler_params=_cparams("arbitrary", "arbitrary"), name="attn_sample",
    )(page_table, q_s, k_new, v_new, bias3, bias3,
      *([cache_k] * ATT_PAGES), *([cache_v] * ATT_PAGES))


def _outproj_kernel(x_ref, att_ref, pool_ref, wo_ref, g_ref, wq_ref, x1_ref, q_ref):
    x1 = x_ref[...] + jnp.dot(att_ref[...], wo_ref[0:ATT_WIDTH, :], preferred_element_type=F32) \
        + jnp.dot(pool_ref[...], wo_ref[ATT_WIDTH:, :], preferred_element_type=F32)
    x1_ref[...] = x1
    h = _rms(x1, g_ref[...]).astype(BF16)
    q_ref[...] = jnp.dot(h, wq_ref[...], preferred_element_type=F32).astype(BF16)


def _outproj(x, att, pool, wo, g, wq, tm):
    n, d = x.shape
    row = lambda i: (i, 0)
    full = lambda i: (0, 0)
    return pl.pallas_call(
        _outproj_kernel, grid=(n // tm,),
        in_specs=[pl.BlockSpec((tm, d), row), pl.BlockSpec((tm, ATT_WIDTH), row),
                  pl.BlockSpec((tm, POOL_WIDTH), row), pl.BlockSpec(wo.shape, full),
                  pl.BlockSpec((1, d), full), pl.BlockSpec(wq.shape, full)],
        out_specs=(pl.BlockSpec((tm, d), row), pl.BlockSpec((tm, MEM_WIDTH), row)),
        out_shape=(jax.ShapeDtypeStruct((n, d), F32), jax.ShapeDtypeStruct((n, MEM_WIDTH), BF16)),
        compiler_params=_cparams("parallel"), name="outproj",
    )(x, att, pool, wo, g, wq)


def _memkv_kernel(mem_ref, g_ref, wk_ref, wv_ref, kf_ref, vf_ref, kb_ref, vb_ref):
    m = _rms(mem_ref[...], g_ref[...]).astype(BF16)
    k = jnp.dot(m, wk_ref[...], preferred_element_type=F32)
    v = jnp.dot(m, wv_ref[...], preferred_element_type=F32)
    kf_ref[...] = k
    vf_ref[...] = v
    kb_ref[...] = k.astype(BF16)
    vb_ref[...] = v.astype(BF16)


def _memkv(mem, g, wk, wv, tm):
    n, d = mem.shape
    row = lambda i: (i, 0)
    full = lambda i: (0, 0)
    spec = pl.BlockSpec((tm, MEM_WIDTH), row)
    return pl.pallas_call(
        _memkv_kernel, grid=(n // tm,),
        in_specs=[pl.BlockSpec((tm, d), row), pl.BlockSpec((1, d), full),
                  pl.BlockSpec(wk.shape, full), pl.BlockSpec(wv.shape, full)],
        out_specs=(spec, spec, spec, spec),
        out_shape=(jax.ShapeDtypeStruct((n, MEM_WIDTH), F32), jax.ShapeDtypeStruct((n, MEM_WIDTH), F32),
                   jax.ShapeDtypeStruct((n, MEM_WIDTH), BF16), jax.ShapeDtypeStruct((n, MEM_WIDTH), BF16)),
        compiler_params=_cparams("parallel"), name="memkv",
    )(mem, g, wk, wv)


def _cross_prompt_kernel(q_ref, mk_ref, mv_ref, o_ref):
    nt = (((1,), (1,)), ((), ()))
    for h in range(N_MEM_HEADS):
        sl = slice(h * MEM_HEAD_DIM, (h + 1) * MEM_HEAD_DIM)
        s = lax.dot_general(q_ref[:, sl], mk_ref[:, sl], nt, preferred_element_type=F32)
        s = s * (MEM_HEAD_DIM ** -0.5)
        e = jnp.exp(s - jnp.max(s, axis=1, keepdims=True))
        p = e / jnp.sum(e, axis=1, keepdims=True)
        o_ref[:, sl] = jnp.dot(p.astype(BF16), mv_ref[:, sl], preferred_element_type=F32).astype(BF16)


def _cross_prompt(q, mk, mv, batch, tm):
    n = q.shape[0]
    nt = n // batch // tm
    n_mem = mk.shape[0] // batch
    blk = lambda b, i: (b * nt + i, 0)
    per_batch = lambda b, i: (b, 0)
    return pl.pallas_call(
        _cross_prompt_kernel, grid=(batch, nt),
        in_specs=[pl.BlockSpec((tm, MEM_WIDTH), blk), pl.BlockSpec((n_mem, MEM_WIDTH), per_batch),
                  pl.BlockSpec((n_mem, MEM_WIDTH), per_batch)],
        out_specs=pl.BlockSpec((tm, MEM_WIDTH), blk),
        out_shape=jax.ShapeDtypeStruct((n, MEM_WIDTH), BF16),
        compiler_params=_cparams("parallel", "parallel"), name="cross_prompt",
    )(q, mk, mv)


def _cross_sample_kernel(q_ref, mk_ref, mv_ref, o_ref):
    hm = _head_mask(N_MEM_HEADS, MEM_HEAD_DIM)
    qbd = jnp.where(hm, jnp.broadcast_to(q_ref[0].astype(F32), hm.shape), 0.0).astype(BF16)
    nt = (((1,), (1,)), ((), ()))
    s = lax.dot_general(qbd, mk_ref[0].astype(BF16), nt, preferred_element_type=F32)
    s = s * (MEM_HEAD_DIM ** -0.5)
    e = jnp.exp(s - jnp.max(s, axis=1, keepdims=True))
    p = e / jnp.sum(e, axis=1, keepdims=True)
    o = jnp.dot(p.astype(BF16), mv_ref[0].astype(BF16), preferred_element_type=F32)
    o_ref[0] = jnp.sum(jnp.where(hm, o, 0.0), axis=0, keepdims=True).astype(BF16)


def _cross_sample(q, mem_k, mem_v):
    db, n_mem, _ = mem_k.shape
    tok = lambda b: (b, 0, 0)
    return pl.pallas_call(
        _cross_sample_kernel, grid=(db,),
        in_specs=[pl.BlockSpec((1, 1, MEM_WIDTH), tok), pl.BlockSpec((1, n_mem, MEM_WIDTH), tok),
                  pl.BlockSpec((1, n_mem, MEM_WIDTH), tok)],
        out_specs=pl.BlockSpec((1, 1, MEM_WIDTH), tok),
        out_shape=jax.ShapeDtypeStruct((db, 1, MEM_WIDTH), BF16),
        compiler_params=_cparams("parallel"), name="cross_sample",
    )(q.reshape(db, 1, MEM_WIDTH), mem_k, mem_v)


def _top_rows(s, k):
    rows = s.shape[0]
    row = lax.broadcasted_iota(I32, s.shape, 0)
    vals, idxs = [], []
    for _ in range(k):
        m = jnp.max(s, axis=0, keepdims=True)
        am = jnp.min(jnp.where(s == m, row, rows), axis=0, keepdims=True)
        vals.append(m)
        idxs.append(am)
        s = jnp.where(row == am, -jnp.inf, s)
    return jnp.concatenate(vals, axis=0), jnp.concatenate(idxs, axis=0)


_CAND_ROWS = PEER_TOPK + 8 * 7 + 8


def _route_kernel(x1_ref, o_ref, wmo_ref, g_ref, wpq_ref, sk_ref,
                  x2_ref, h_ref, eid_ref, gate_ref, pq_ref, ts_ref, ti_ref):
    tm = x1_ref.shape[0]
    x2 = x1_ref[...] + jnp.dot(o_ref[...], wmo_ref[...], preferred_element_type=F32)
    x2_ref[...] = x2
    hf = _rms(x2, g_ref[...])
    h_ref[...] = hf
    hb = hf.astype(BF16)
    for hp in range(2 * PEER_HEADS):
        cols = slice(hp * PEER_HALF_DIM, (hp + 1) * PEER_HALF_DIM)
        pq_ref[hp] = jnp.dot(hb, wpq_ref[:, cols], preferred_element_type=F32).astype(BF16)
    nt = (((1,), (1,)), ((), ()))

    def half_topk(hp, carry):
        st = lax.dot_general(sk_ref[hp], pq_ref[hp], nt, preferred_element_type=F32)
        vals, idxs = _top_rows(st, PEER_TOPK)
        ts_ref[hp] = vals
        ti_ref[hp] = idxs
        return carry

    lax.fori_loop(0, 2 * PEER_HEADS, half_topk, 0)

    grp = lax.broadcasted_iota(I32, (8, tm), 0)

    def head_select(h, carry):
        s1, s2 = ts_ref[2 * h], ts_ref[2 * h + 1]
        i1, i2 = ti_ref[2 * h], ti_ref[2 * h + 1]
        cs = [s1[0:1] + s2]
        ce = [i1[0:1] * PEER_N_KEYS + i2]
        for i in range(1, 8):
            valid = grp < (PEER_TOPK // (i + 1))
            cs.append(jnp.where(valid, s1[i:i + 1] + s2[0:8], -jnp.inf))
            ce.append(i1[i:i + 1] * PEER_N_KEYS + i2[0:8])
        cs.append(s1[8:16] + s2[0:1])
        ce.append(i1[8:16] * PEER_N_KEYS + i2[0:1])
        cand_s = jnp.concatenate(cs, axis=0)
        cand_e = jnp.concatenate(ce, axis=0)
        row = lax.broadcasted_iota(I32, cand_s.shape, 0)
        top_s, top_e = [], []
        for _ in range(PEER_TOPK):
            m = jnp.max(cand_s, axis=0, keepdims=True)
            am = jnp.min(jnp.where(cand_s == m, row, _CAND_ROWS), axis=0, keepdims=True)
            hit = row == am
            top_s.append(m)
            top_e.append(jnp.max(jnp.where(hit, cand_e, -1), axis=0, keepdims=True))
            cand_s = jnp.where(hit, -jnp.inf, cand_s)
        top_s = jnp.concatenate(top_s, axis=0)
        e = jnp.exp(top_s - top_s[0:1])
        gate = e / jnp.sum(e, axis=0, keepdims=True)
        r0 = pl.multiple_of(h * PEER_TOPK, PEER_TOPK)
        eid_ref[pl.ds(r0, PEER_TOPK), :] = jnp.concatenate(top_e, axis=0)
        gate_ref[pl.ds(r0, PEER_TOPK), :] = gate
        return carry

    lax.fori_loop(0, PEER_HEADS, head_select, 0)


def _route(x1, o, wmo, g, wpq, sk, tm):
    n, d = x1.shape
    row = lambda i: (i, 0)
    full = lambda i: (0, 0)
    col = lambda i: (0, i)
    return pl.pallas_call(
        _route_kernel, grid=(n // tm,),
        in_specs=[pl.BlockSpec((tm, d), row), pl.BlockSpec((tm, MEM_WIDTH), row),
                  pl.BlockSpec(wmo.shape, full), pl.BlockSpec((1, d), full),
                  pl.BlockSpec(wpq.shape, full), pl.BlockSpec(sk.shape, lambda i: (0, 0, 0))],
        out_specs=(pl.BlockSpec((tm, d), row), pl.BlockSpec((tm, d), row),
                   pl.BlockSpec((N_SEL, tm), col), pl.BlockSpec((N_SEL, tm), col)),
        out_shape=(jax.ShapeDtypeStruct((n, d), F32), jax.ShapeDtypeStruct((n, d), F32),
                   jax.ShapeDtypeStruct((N_SEL, n), I32), jax.ShapeDtypeStruct((N_SEL, n), F32)),
        scratch_shapes=[pltpu.VMEM((2 * PEER_HEADS, tm, PEER_HALF_DIM), BF16),
                        pltpu.VMEM((2 * PEER_HEADS, PEER_TOPK, tm), F32),
                        pltpu.VMEM((2 * PEER_HEADS, PEER_TOPK, tm), I32)],
        compiler_params=_cparams("parallel"), name="route",
    )(x1, o, wmo, g, wpq, sk)


SC_WINDOW = 32
SC_CORES = 2
SC_SUBCORES = 16


def _sc_gather(table, idx):
    m = idx.shape[0]
    width = table.shape[1]
    n_workers = SC_CORES * SC_SUBCORES
    per_worker = m // n_workers
    steps = per_worker // SC_WINDOW
    mesh = plsc.VectorSubcoreMesh(core_axis_name="c", subcore_axis_name="s")

    @functools.partial(
        pl.kernel, mesh=mesh,
        out_type=jax.ShapeDtypeStruct((m, width), table.dtype),
        scratch_types=[pltpu.VMEM((SC_WINDOW,), I32),
                       pltpu.VMEM((SC_WINDOW, width), table.dtype),
                       pltpu.SemaphoreType.DMA])
    def gather(table_hbm, idx_hbm, out_hbm, idx_v, rows_v, sem):
        wid = lax.axis_index("s") * SC_CORES + lax.axis_index("c")
        base = wid * per_worker

        @pl.loop(0, steps)
        def _(t):
            off = base + t * SC_WINDOW
            pltpu.sync_copy(idx_hbm.at[pl.ds(off, SC_WINDOW)], idx_v)
            pltpu.async_copy(table_hbm.at[idx_v], rows_v, sem).wait()
            pltpu.sync_copy(rows_v, out_hbm.at[pl.ds(off, SC_WINDOW)])

    return gather(table, idx)


PEER_TT = 8


def _peer_kernel(x2_ref, h_ref, gate_ref, u_ref, v_ref, g_ref, y_ref):
    tt, d = x2_ref.shape
    col = lax.broadcasted_iota(I32, (N_SEL, tt), 1)
    act = jnp.zeros((N_SEL, tt), F32)
    for t in range(tt):
        u = u_ref[t * N_SEL:(t + 1) * N_SEL, :]
        a_t = jnp.sum(u * h_ref[t:t + 1, :], axis=1, keepdims=True)
        act = jnp.where(col == t, a_t, act)
    gelu = 0.5 * act * (1.0 + lax.erf(act * (2.0 ** -0.5)))
    coef = gate_ref[0] * gelu
    row = lax.broadcasted_iota(I32, (tt, d), 0)
    out = jnp.zeros((tt, d), F32)
    for t in range(tt):
        v = v_ref[t * N_SEL:(t + 1) * N_SEL, :]
        o_t = jnp.sum(coef[:, t:t + 1] * v, axis=0, keepdims=True)
        out = jnp.where(row == t, o_t, out)
    y_ref[...] = _rms(x2_ref[...] + out, g_ref[...])


def _peer(x2, h, gate_blk, ug, vg, g):
    n, d = x2.shape
    tt = PEER_TT
    row = lambda i: (i, 0)
    return pl.pallas_call(
        _peer_kernel, grid=(n // tt,),
        in_specs=[pl.BlockSpec((tt, d), row), pl.BlockSpec((tt, d), row),
                  pl.BlockSpec((1, N_SEL, tt), lambda i: (i, 0, 0)),
                  pl.BlockSpec((tt * N_SEL, d), row),
                  pl.BlockSpec((tt * N_SEL, d), row), pl.BlockSpec((1, d), lambda i: (0, 0))],
        out_specs=pl.BlockSpec((tt, d), row),
        out_shape=jax.ShapeDtypeStruct((n, d), F32),
        compiler_params=_cparams("parallel"), name="peer",
    )(x2, h, gate_blk, ug, vg, g)


PEER_CHUNK = 4096


def _peer_ffn_final(x2, h, eid_t, gate_t, peer_u, peer_v, g_final):
    n = x2.shape[0]
    eid = eid_t.T
    gate_blk = gate_t.reshape(N_SEL, n // PEER_TT, PEER_TT).transpose(1, 0, 2)
    ys = []
    for c0 in range(0, n, PEER_CHUNK):
        c1 = min(n, c0 + PEER_CHUNK)
        flat = eid[c0:c1].reshape(-1)
        ug = _sc_gather(peer_u, flat)
        vg = _sc_gather(peer_v, flat)
        ys.append(_peer(x2[c0:c1], h[c0:c1], gate_blk[c0 // PEER_TT:c1 // PEER_TT], ug, vg, g_final))
    return ys[0] if len(ys) == 1 else jnp.concatenate(ys, axis=0)


def _rope_tables(first_pos, t):
    half = HEAD_DIM // 2
    inv_freq = ROPE_THETA ** (-jnp.arange(half, dtype=F32) / half)
    pos = first_pos + jnp.arange(t, dtype=I32)
    ang = pos.astype(F32)[:, None] * inv_freq[None, :]
    cos, sin = jnp.cos(ang), jnp.sin(ang)
    cos_t = jnp.concatenate([cos, cos, cos, cos], axis=1)
    sin_t = jnp.concatenate([-sin, sin, -sin, sin], axis=1)
    return cos_t, sin_t


def _row_tile(n, pref):
    return pref if n % pref == 0 else n


def kernel(x_prompt, x_sample, mem_prompt, cache_k, cache_v, cache_idx_k, state_pool, cache_mem_k,
           cache_mem_v, page_table, norm_mix, w_in, pool_w, pool_scale, w_out, norm_cross, norm_mem,
           w_mq, w_mk, w_mv, w_mo, norm_ffn, w_pq, sub_keys, peer_u, peer_v, norm_final):
    batch, seq, d = x_prompt.shape
    db, ds, _ = x_sample.shape
    depth = w_in.shape[0]
    assert depth == 1 and ds == 1
    n_pages = page_table.shape[1]
    past = n_pages * PAGE_SIZE
    n_mem = mem_prompt.shape[1]
    n_p, n_s = batch * seq, db * ds
    l = 0

    w = w_in[l]
    c_ki = 4 * ATT_WIDTH
    c_wi = c_ki + IDX_HEAD_DIM
    c_p = c_wi + N_IDX_HEADS
    wbig = jnp.concatenate([w[:, :c_ki], w[:, c_p:]], axis=1).astype(BF16)
    wsm = jnp.pad(w[:, c_ki:c_p], ((0, 0), (0, LANES - (c_p - c_ki)))).astype(BF16)
    g_mix = norm_mix[l][None, :]
    pw = pool_w[l].astype(BF16)
    ps = pool_scale[l][None, :]
    wo = w_out[l].astype(BF16)
    g_cross = norm_cross[l][None, :]
    wq = w_mq[l].astype(BF16)
    wmo = w_mo[l].astype(BF16)
    g_ffn = norm_ffn[l][None, :]
    wpq = w_pq[l].astype(BF16)
    sk = sub_keys[l].reshape(2 * PEER_HEADS, PEER_N_KEYS, PEER_HALF_DIM).astype(BF16)
    g_final = norm_final[None, :]
    pu, pv = peer_u[l], peer_v[l]

    tm = _row_tile(seq, 512)
    cos_p, sin_p = _rope_tables(0, seq)
    xp = x_prompt.reshape(n_p, d)
    q, kf, kb, vf, vb, qi, kif, kib, wi, p = _inproj(xp, g_mix, wbig, wsm, cos_p, sin_p, tm)
    att = _dsa_prompt(q, qi, wi, kb, vb, kib, batch, seq)
    pool_out = _pool_prompt(p, pw, ps, batch, tm)
    mkf, mvf, mkb, mvb = _memkv(mem_prompt.reshape(batch * n_mem, d), norm_mem[l][None, :],
                                w_mk[l].astype(BF16), w_mv[l].astype(BF16), _row_tile(batch * n_mem, 256))
    x1, qm = _outproj(xp, att, pool_out, wo, g_cross, wq, tm)
    o = _cross_prompt(qm, mkb, mvb, batch, tm)
    x2, h, eid_t, gate_t = _route(x1, o, wmo, g_ffn, wpq, sk, _row_tile(n_p, 256))
    y_prompt = _peer_ffn_final(x2, h, eid_t, gate_t, pu, pv, g_final).reshape(batch, seq, d)

    new_k_prompt = kf.reshape(1, batch, seq, N_HEADS, HEAD_DIM)
    new_v_prompt = vf.reshape(1, batch, seq, N_HEADS, HEAD_DIM)
    new_idx_k_prompt = kif.reshape(1, batch, seq, IDX_HEAD_DIM)
    new_pool_prompt = p.reshape(batch, seq, POOL_WIDTH)[None, :, seq - POOL_STATE_LEN:, :]
    new_mem_k_prompt = mkf.reshape(1, batch, n_mem, N_MEM_HEADS, MEM_HEAD_DIM)
    new_mem_v_prompt = mvf.reshape(1, batch, n_mem, N_MEM_HEADS, MEM_HEAD_DIM)

    cos_s, sin_s = _rope_tables(past, 1)
    cos_s = jnp.broadcast_to(cos_s, (n_s, LANES))
    sin_s = jnp.broadcast_to(sin_s, (n_s, LANES))
    xs = x_sample.reshape(n_s, d)
    q, kf, kb, vf, vb, qi, kif, kib, wi, p = _inproj(xs, g_mix, wbig, wsm, cos_s, sin_s, n_s)
    n_pool = cache_k.shape[1]
    ck = cache_k[l].reshape(n_pool, PAGE_SIZE, ATT_WIDTH)
    cv = cache_v[l].reshape(n_pool, PAGE_SIZE, ATT_WIDTH)
    qi_s = jnp.transpose(qi, (1, 0, 2))
    scores, score_new = _idx_sample(page_table, qi_s, wi.reshape(n_s, N_IDX_HEADS, 1),
                                    kib.reshape(n_s, 1, IDX_HEAD_DIM), cache_idx_k[l])
    k_top = min(TOPK_MAX, (past + ds) // 4)
    bias = _bias_sample(scores.reshape(n_s, past), score_new.reshape(n_s, LANES), k_top)
    att = _attn_sample(page_table, q.reshape(n_s, 1, ATT_WIDTH), kf.reshape(n_s, 1, ATT_WIDTH),
                       vf.reshape(n_s, 1, ATT_WIDTH), bias, ck, cv).reshape(n_s, ATT_WIDTH)
    state_t = jnp.transpose(state_pool[l], (1, 0, 2))
    pool_out = _pool_sample(state_t, p, pw, ps)
    x1, qm = _outproj(xs, att, pool_out, wo, g_cross, wq, n_s)
    o = _cross_sample(qm, cache_mem_k[l].reshape(db, n_mem, MEM_WIDTH),
                      cache_mem_v[l].reshape(db, n_mem, MEM_WIDTH)).reshape(n_s, MEM_WIDTH)
    x2, h, eid_t, gate_t = _route(x1, o, wmo, g_ffn, wpq, sk, n_s)
    y_sample = _peer_ffn_final(x2, h, eid_t, gate_t, pu, pv, g_final).reshape(db, ds, d)

    new_k_sample = kf.reshape(1, db, ds, N_HEADS, HEAD_DIM)
    new_v_sample = vf.reshape(1, db, ds, N_HEADS, HEAD_DIM)
    new_idx_k_sample = kif.reshape(1, db, ds, IDX_HEAD_DIM)
    new_pool_sample = jnp.concatenate([state_pool[l][:, 1:, :], p[:, None, :]], axis=1)[None]

    return (y_prompt, y_sample, new_k_prompt, new_v_prompt, new_idx_k_prompt, new_pool_prompt,
            new_mem_k_prompt, new_mem_v_prompt, new_k_sample, new_v_sample, new_idx_k_sample,
            new_pool_sample)
```

```python
import functools

import numpy as np
import jax
import jax.numpy as jnp
from jax import lax
from jax.experimental import pallas as pl
from jax.experimental.pallas import tpu as pltpu
from jax.experimental.pallas import tpu_sc as plsc

F32 = jnp.float32
BF16 = jnp.bfloat16
I32 = jnp.int32

N_HEADS = 8
HEAD_DIM = 64
ATT_WIDTH = N_HEADS * HEAD_DIM
N_IDX_HEADS = 8
IDX_HEAD_DIM = 64
TOPK_MAX = 256
POOL_WINDOWS = (2, 4, 8, 16)
POOL_GROUP_DIM = 128
POOL_WIDTH = POOL_GROUP_DIM * len(POOL_WINDOWS)
POOL_STATE_LEN = max(POOL_WINDOWS) - 1
N_MEM_HEADS = 4
MEM_HEAD_DIM = 128
MEM_WIDTH = N_MEM_HEADS * MEM_HEAD_DIM
PEER_HEADS = 8
PEER_N_KEYS = 128
PEER_HALF_DIM = 128
PEER_TOPK = 16
N_SEL = PEER_HEADS * PEER_TOPK
PAGE_SIZE = 128
ROPE_THETA = 10000.0
RMS_EPS = 1e-6

LANES = 128
MASK_VALUE = -1e30
INT_MIN = np.int32(-2 ** 31)
NEG_INF_KEY = np.int32(np.uint32(0x807FFFFF).astype(np.int64) - 2 ** 32)

VMEM_LIMIT = 56 * 1024 * 1024


def _cparams(*sem):
    return pltpu.CompilerParams(dimension_semantics=sem, vmem_limit_bytes=VMEM_LIMIT)


def _rms(x, g):
    ms = jnp.mean(x * x, axis=-1, keepdims=True)
    return x * lax.rsqrt(ms + RMS_EPS) * g


def _inproj_kernel(x_ref, g_ref, wbig_ref, wsm_ref, cos_ref, sin_ref,
                   q_ref, kf_ref, kb_ref, vf_ref, vb_ref, qi_ref, kif_ref, kib_ref, wi_ref, p_ref):
    h = _rms(x_ref[...], g_ref[...]).astype(BF16)
    cos = cos_ref[...]
    sin = sin_ref[...]
    lane = lax.broadcasted_iota(I32, (1, LANES), 1)
    first_half = (lane % HEAD_DIM) < (HEAD_DIM // 2)

    def rope(z):
        partner = jnp.where(first_half, pltpu.roll(z, LANES - HEAD_DIM // 2, 1),
                            pltpu.roll(z, HEAD_DIM // 2, 1))
        return z * cos + partner * sin

    def proj(c0, width):
        return jnp.dot(h, wbig_ref[:, c0:c0 + width], preferred_element_type=F32)

    for s in range(ATT_WIDTH // LANES):
        sl = slice(s * LANES, (s + 1) * LANES)
        zq = rope(proj(s * LANES, LANES))
        q_ref[:, sl] = (zq * (HEAD_DIM ** -0.5)).astype(BF16)
        zk = rope(proj(ATT_WIDTH + s * LANES, LANES))
        kf_ref[:, sl] = zk
        kb_ref[:, sl] = zk.astype(BF16)
        zqi = rope(proj(3 * ATT_WIDTH + s * LANES, LANES)).astype(BF16)
        qi_ref[2 * s] = zqi[:, :IDX_HEAD_DIM]
        qi_ref[2 * s + 1] = zqi[:, IDX_HEAD_DIM:]
    zv = proj(2 * ATT_WIDTH, ATT_WIDTH)
    vf_ref[...] = zv
    vb_ref[...] = zv.astype(BF16)
    p_ref[...] = proj(4 * ATT_WIDTH, POOL_WIDTH)
    zs = jnp.dot(h, wsm_ref[...], preferred_element_type=F32)
    zki = rope(zs)[:, :IDX_HEAD_DIM]
    kif_ref[...] = zki
    kib_ref[...] = zki.astype(BF16)
    wi_ref[...] = zs[:, IDX_HEAD_DIM:IDX_HEAD_DIM + N_IDX_HEADS] * (
        (N_IDX_HEADS ** -0.5) * (IDX_HEAD_DIM ** -0.5))


def _inproj(x, g, wbig, wsm, cos, sin, tm):
    n, d = x.shape
    nt = n // tm
    nrt = cos.shape[0] // tm
    row = lambda i: (i, 0)
    full = lambda i: (0, 0)
    out_shape = (
        jax.ShapeDtypeStruct((n, ATT_WIDTH), BF16),
        jax.ShapeDtypeStruct((n, ATT_WIDTH), F32),
        jax.ShapeDtypeStruct((n, ATT_WIDTH), BF16),
        jax.ShapeDtypeStruct((n, ATT_WIDTH), F32),
        jax.ShapeDtypeStruct((n, ATT_WIDTH), BF16),
        jax.ShapeDtypeStruct((N_IDX_HEADS, n, IDX_HEAD_DIM), BF16),
        jax.ShapeDtypeStruct((n, IDX_HEAD_DIM), F32),
        jax.ShapeDtypeStruct((n, IDX_HEAD_DIM), BF16),
        jax.ShapeDtypeStruct((n, N_IDX_HEADS), F32),
        jax.ShapeDtypeStruct((n, POOL_WIDTH), F32),
    )
    out_specs = (
        pl.BlockSpec((tm, ATT_WIDTH), row), pl.BlockSpec((tm, ATT_WIDTH), row),
        pl.BlockSpec((tm, ATT_WIDTH), row), pl.BlockSpec((tm, ATT_WIDTH), row),
        pl.BlockSpec((tm, ATT_WIDTH), row),
        pl.BlockSpec((N_IDX_HEADS, tm, IDX_HEAD_DIM), lambda i: (0, i, 0)),
        pl.BlockSpec((tm, IDX_HEAD_DIM), row), pl.BlockSpec((tm, IDX_HEAD_DIM), row),
        pl.BlockSpec((tm, N_IDX_HEADS), row), pl.BlockSpec((tm, POOL_WIDTH), row),
    )
    return pl.pallas_call(
        _inproj_kernel, grid=(nt,),
        in_specs=[pl.BlockSpec((tm, d), row), pl.BlockSpec((1, d), full),
                  pl.BlockSpec(wbig.shape, full), pl.BlockSpec(wsm.shape, full),
                  pl.BlockSpec((tm, LANES), lambda i: (i % nrt, 0)),
                  pl.BlockSpec((tm, LANES), lambda i: (i % nrt, 0))],
        out_specs=out_specs, out_shape=out_shape,
        compiler_params=_cparams("parallel"), name="inproj",
    )(x, g, wbig, wsm, cos, sin)


HALO = 16


def _pool_prompt_kernel(p_ref, pw_ref, ps_ref, o_ref, ext_ref):
    tm = p_ref.shape[0]
    i = pl.program_id(1)

    @pl.when(i == 0)
    def _():
        ext_ref[0:HALO, :] = jnp.zeros((HALO, POOL_WIDTH), F32)

    @pl.when(i > 0)
    def _():
        ext_ref[0:HALO, :] = ext_ref[tm:tm + HALO, :]

    ext_ref[HALO:HALO + tm, :] = p_ref[...]
    pos = i * tm + lax.broadcasted_iota(I32, (tm, 1), 0)
    for g, w in enumerate(POOL_WINDOWS):
        sl = slice(g * POOL_GROUP_DIM, (g + 1) * POOL_GROUP_DIM)
        cur = ext_ref[HALO:HALO + tm, sl]
        s = cur
        for j in range(1, w):
            s = s + ext_ref[HALO - j:HALO - j + tm, sl]
        cnt = jnp.minimum(w, pos + 1).astype(F32)
        pooled = s / cnt - cur
        mixed = jnp.dot(pooled.astype(BF16), pw_ref[g], preferred_element_type=F32)
        o_ref[:, sl] = (mixed * ps_ref[:, sl]).astype(BF16)


def _pool_prompt(p, pw, ps, batch, tm):
    n = p.shape[0]
    nt = n // batch // tm
    return pl.pallas_call(
        _pool_prompt_kernel, grid=(batch, nt),
        in_specs=[pl.BlockSpec((tm, POOL_WIDTH), lambda b, i: (b * nt + i, 0)),
                  pl.BlockSpec(pw.shape, lambda b, i: (0, 0, 0)),
                  pl.BlockSpec((1, POOL_WIDTH), lambda b, i: (0, 0))],
        out_specs=pl.BlockSpec((tm, POOL_WIDTH), lambda b, i: (b * nt + i, 0)),
        out_shape=jax.ShapeDtypeStruct((n, POOL_WIDTH), BF16),
        scratch_shapes=[pltpu.VMEM((HALO + tm, POOL_WIDTH), F32)],
        compiler_params=_cparams("arbitrary", "arbitrary"), name="pool_prompt",
    )(p, pw, ps)


def _pool_sample_kernel(st_ref, p_ref, pw_ref, ps_ref, o_ref):
    for g, w in enumerate(POOL_WINDOWS):
        sl = slice(g * POOL_GROUP_DIM, (g + 1) * POOL_GROUP_DIM)
        cur = p_ref[:, sl]
        s = cur
        for j in range(1, w):
            s = s + st_ref[POOL_STATE_LEN - j, :, sl]
        pooled = s / float(w) - cur
        mixed = jnp.dot(pooled.astype(BF16), pw_ref[g], preferred_element_type=F32)
        o_ref[:, sl] = (mixed * ps_ref[:, sl]).astype(BF16)


def _pool_sample(state_t, p, pw, ps):
    n = p.shape[0]
    return pl.pallas_call(
        _pool_sample_kernel,
        out_shape=jax.ShapeDtypeStruct((n, POOL_WIDTH), BF16),
        compiler_params=pltpu.CompilerParams(vmem_limit_bytes=VMEM_LIMIT), name="pool_sample",
    )(state_t, p, pw, ps)


def _sort_key(x):
    b = lax.bitcast_convert_type(x, I32)
    return b ^ ((b >> 31) & np.int32(0x7FFFFFFF))


def _count_ge(skey_ref, n_chunks, chunk, cand):
    rows = skey_ref.shape[0]
    candb = jnp.broadcast_to(cand, (rows, LANES))

    def body(c, acc):
        off = pl.multiple_of(c * chunk, chunk)
        blk = skey_ref[:, pl.ds(off, chunk)]
        for j in range(chunk // LANES):
            acc = acc + jnp.where(blk[:, j * LANES:(j + 1) * LANES] >= candb, 1.0, 0.0)
        return acc

    acc = lax.fori_loop(0, n_chunks, body, jnp.zeros((rows, LANES), F32))
    return jnp.sum(acc, axis=1, keepdims=True)


def _select_threshold(skey_ref, n_chunks, chunk, k_top):
    rows = skey_ref.shape[0]
    kf = float(k_top)

    def bit_step(i, prefix):
        cand_u = prefix | lax.shift_left(jnp.int32(1), jnp.asarray(31 - i, I32))
        cnt = _count_ge(skey_ref, n_chunks, chunk, cand_u ^ INT_MIN)
        return jnp.where(cnt >= kf, cand_u, prefix)

    prefix = lax.fori_loop(0, 32, bit_step, jnp.zeros((rows, 1), I32))
    tau = prefix ^ INT_MIN
    n_ge = _count_ge(skey_ref, n_chunks, chunk, tau)
    tie_row = (n_ge > kf) & (tau > NEG_INF_KEY)
    any_tie = jnp.max(jnp.where(tie_row, 1.0, 0.0)) > 0.0

    @pl.when(any_tie)
    def _():
        n_gt = _count_ge(skey_ref, n_chunks, chunk, tau + 1)
        need = kf - n_gt
        r = lax.broadcasted_iota(I32, (chunk, chunk), 0)
        c = lax.broadcasted_iota(I32, (chunk, chunk), 1)
        before = jnp.where(r < c, 1.0, 0.0).astype(BF16)

        def body(ci, seen):
            off = pl.multiple_of(ci * chunk, chunk)
            blk = skey_ref[:, pl.ds(off, chunk)]
            eq = blk == tau
            eqf = jnp.where(eq, 1.0, 0.0)
            rank = seen + jnp.dot(eqf.astype(BF16), before, preferred_element_type=F32)
            drop = eq & (rank >= need) & tie_row
            skey_ref[:, pl.ds(off, chunk)] = jnp.where(drop, tau - 1, blk)
            return seen + jnp.sum(eqf, axis=1, keepdims=True)

        lax.fori_loop(0, n_chunks, body, jnp.zeros((rows, 1), F32))

    return jnp.maximum(tau, NEG_INF_KEY + 1)


DSA_TQ = 128
DSA_TK = 512


def _dsa_prompt_kernel(k_top, q_ref, qi_ref, wi_ref, k_ref, v_ref, ki_ref, o_ref,
                       skey_ref, qm_ref, m_ref, l_ref, acc_ref):
    tq, tk = DSA_TQ, DSA_TK
    qb = pl.program_id(1)
    n_chunks = (qb * tq) // tk + 1
    t_row = qb * tq + lax.broadcasted_iota(I32, (tq, 1), 0)

    qi = qi_ref[...].reshape(N_IDX_HEADS * tq, IDX_HEAD_DIM)
    wi = wi_ref[...]

    def score_chunk(c, carry):
        off = pl.multiple_of(c * tk, tk)
        dots = lax.dot_general(qi, ki_ref[pl.ds(off, tk), :], (((1,), (1,)), ((), ())),
                               preferred_element_type=F32)
        sc = None
        for h in range(N_IDX_HEADS):
            term = jnp.maximum(dots[h * tq:(h + 1) * tq], 0.0) * wi[:, h:h + 1]
            sc = term if sc is None else sc + term
        key_pos = off + lax.broadcasted_iota(I32, (1, tk), 1)
        sc = jnp.where(key_pos <= t_row, sc, -jnp.inf)
        skey_ref[:, pl.ds(off, tk)] = _sort_key(sc)
        return carry

    lax.fori_loop(0, n_chunks, score_chunk, 0)

    thr = _select_threshold(skey_ref, n_chunks, tk, k_top)

    lane = lax.broadcasted_iota(I32, (1, LANES), 1)
    low = lane < HEAD_DIM
    n_pairs = N_HEADS // 2
    for p in range(n_pairs):
        slab = q_ref[:, p * LANES:(p + 1) * LANES]
        qm_ref[p, 0:tq, :] = jnp.where(low, slab, jnp.zeros_like(slab))
        qm_ref[p, tq:2 * tq, :] = jnp.where(low, jnp.zeros_like(slab), slab)
    m_ref[...] = jnp.full(m_ref.shape, MASK_VALUE, F32)
    l_ref[...] = jnp.zeros(l_ref.shape, F32)
    acc_ref[...] = jnp.zeros(acc_ref.shape, F32)

    def attn_chunk(c, carry):
        off = pl.multiple_of(c * tk, tk)
        bias = jnp.where(skey_ref[:, pl.ds(off, tk)] >= thr, 0.0, MASK_VALUE)
        for p in range(n_pairs):
            sl = slice(p * LANES, (p + 1) * LANES)
            s2 = lax.dot_general(qm_ref[p], k_ref[pl.ds(off, tk), sl], (((1,), (1,)), ((), ())),
                                 preferred_element_type=F32)
            probs = []
            for hh in range(2):
                rs = slice(hh * tq, (hh + 1) * tq)
                s = s2[rs] + bias
                m_old = m_ref[p, rs, :]
                m_new = jnp.maximum(m_old, jnp.max(s, axis=1, keepdims=True))
                alpha = jnp.exp(m_old - m_new)
                lsum = alpha * l_ref[p, rs, :]
                parts = []
                for j in range(tk // LANES):
                    e = jnp.exp(s[:, j * LANES:(j + 1) * LANES] - m_new)
                    lsum = lsum + e
                    parts.append(e.astype(BF16))
                m_ref[p, rs, :] = m_new
                l_ref[p, rs, :] = lsum
                acc_ref[p, rs, :] = alpha * acc_ref[p, rs, :]
                probs.append(jnp.concatenate(parts, axis=1))
            pv = jnp.dot(jnp.concatenate(probs, axis=0), v_ref[pl.ds(off, tk), sl],
                         preferred_element_type=F32)
            acc_ref[p] = acc_ref[p] + pv
        return carry

    lax.fori_loop(0, n_chunks, attn_chunk, 0)

    for p in range(n_pairs):
        outs = []
        for hh in range(2):
            rs = slice(hh * tq, (hh + 1) * tq)
            denom = jnp.sum(l_ref[p, rs, :], axis=1, keepdims=True)
            outs.append(acc_ref[p, rs, :] / denom)
        o_ref[:, p * LANES:(p + 1) * LANES] = jnp.where(low, outs[0], outs[1]).astype(BF16)


def _dsa_prompt(q, qi, wi, kb, vb, kib, b0, batch, seq):
    nq = seq // DSA_TQ
    k_top = min(TOPK_MAX, seq // 4)
    blk = lambda b, i: ((b0 + b) * nq + i, 0)
    per_batch = lambda b, i: (b0 + b, 0)
    n_pairs = N_HEADS // 2
    return pl.pallas_call(
        functools.partial(_dsa_prompt_kernel, k_top), grid=(batch, nq),
        in_specs=[pl.BlockSpec((DSA_TQ, ATT_WIDTH), blk),
                  pl.BlockSpec((N_IDX_HEADS, DSA_TQ, IDX_HEAD_DIM), lambda b, i: (0, (b0 + b) * nq + i, 0)),
                  pl.BlockSpec((DSA_TQ, N_IDX_HEADS), blk),
                  pl.BlockSpec((seq, ATT_WIDTH), per_batch),
                  pl.BlockSpec((seq, ATT_WIDTH), per_batch),
                  pl.BlockSpec((seq, IDX_HEAD_DIM), per_batch)],
        out_specs=pl.BlockSpec((DSA_TQ, ATT_WIDTH), lambda b, i: (b * nq + i, 0)),
        out_shape=jax.ShapeDtypeStruct((batch * seq, ATT_WIDTH), BF16),
        scratch_shapes=[pltpu.VMEM((DSA_TQ, seq), I32),
                        pltpu.VMEM((n_pairs, 2 * DSA_TQ, LANES), BF16),
                        pltpu.VMEM((n_pairs, 2 * DSA_TQ, LANES), F32),
                        pltpu.VMEM((n_pairs, 2 * DSA_TQ, LANES), F32),
                        pltpu.VMEM((n_pairs, 2 * DSA_TQ, LANES), F32)],
        compiler_params=_cparams("arbitrary", "arbitrary"), name="dsa_prompt",
    )(q, qi, wi, kb, vb, kib)


IDX_PAGES = 16
ATT_PAGES = 8


def _idx_sample_kernel(pt_ref, qi_ref, wi_ref, kin_ref, *refs):
    pages = refs[:IDX_PAGES]
    sc_ref, scn_ref = refs[IDX_PAGES], refs[IDX_PAGES + 1]
    qi = qi_ref[0]
    wi = wi_ref[0]
    nt = (((1,), (1,)), ((), ()))

    def score(keys):
        dots = lax.dot_general(qi, keys, nt, preferred_element_type=F32)
        return jnp.sum(jnp.maximum(dots, 0.0) * wi, axis=0, keepdims=True)

    for j in range(IDX_PAGES):
        sc_ref[0, :, j * PAGE_SIZE:(j + 1) * PAGE_SIZE] = score(pages[j][0].astype(BF16))

    @pl.when(pl.program_id(1) == 0)
    def _():
        new = score(jnp.broadcast_to(kin_ref[0], (8, IDX_HEAD_DIM)))
        lane = lax.broadcasted_iota(I32, (1, LANES), 1)
        scn_ref[0] = jnp.where(lane == 0, jnp.broadcast_to(new[:, 0:1], (1, LANES)), -jnp.inf)


def _idx_sample(page_table, qi_s, wi_s, ki_new, cache_idx_k):
    db, n_pages = page_table.shape
    nj = n_pages // IDX_PAGES
    past = n_pages * PAGE_SIZE

    def page_spec(j):
        return pl.BlockSpec((1, PAGE_SIZE, IDX_HEAD_DIM),
                            lambda b, s, pt, j=j: (pt[b, s * IDX_PAGES + j], 0, 0))

    grid_spec = pltpu.PrefetchScalarGridSpec(
        num_scalar_prefetch=1, grid=(db, nj),
        in_specs=[pl.BlockSpec((1, N_IDX_HEADS, IDX_HEAD_DIM), lambda b, s, pt: (b, 0, 0)),
                  pl.BlockSpec((1, N_IDX_HEADS, 1), lambda b, s, pt: (b, 0, 0)),
                  pl.BlockSpec((1, 1, IDX_HEAD_DIM), lambda b, s, pt: (b, 0, 0))]
                 + [page_spec(j) for j in range(IDX_PAGES)],
        out_specs=[pl.BlockSpec((1, 1, IDX_PAGES * PAGE_SIZE), lambda b, s, pt: (b, 0, s)),
                   pl.BlockSpec((1, 1, LANES), lambda b, s, pt: (b, 0, 0))])
    return pl.pallas_call(
        _idx_sample_kernel, grid_spec=grid_spec,
        out_shape=(jax.ShapeDtypeStruct((db, 1, past), F32),
                   jax.ShapeDtypeStruct((db, 1, LANES), F32)),
        compiler_params=_cparams("arbitrary", "arbitrary"), name="idx_sample",
    )(page_table, qi_s, wi_s, ki_new, *([cache_idx_k] * IDX_PAGES))


def _bias_sample_kernel(k_top, sc_ref, scn_ref, bias_ref, skey_ref, thr_ref):
    past = sc_ref.shape[1]
    width = past + LANES
    ci = pl.program_id(0)

    @pl.when(ci == 0)
    def _():
        skey_ref[:, 0:past] = _sort_key(sc_ref[...])
        skey_ref[:, past:width] = _sort_key(scn_ref[...])
        thr = _select_threshold(skey_ref, width // LANES, LANES, k_top)
        thr_ref[...] = jnp.broadcast_to(thr, thr_ref.shape)

    r = lax.broadcasted_iota(I32, (LANES, LANES * N_HEADS), 0)
    c = lax.broadcasted_iota(I32, (LANES, LANES * N_HEADS), 1)
    expand = jnp.where(c // N_HEADS == r, 1.0, 0.0).astype(BF16)
    off = pl.multiple_of(ci * LANES, LANES)
    bias = jnp.where(skey_ref[:, pl.ds(off, LANES)] >= thr_ref[...], 0.0, MASK_VALUE).astype(BF16)
    bias_ref[0] = jnp.dot(bias, expand, preferred_element_type=F32)


def _bias_sample(scores, score_new, k_top):
    db, past = scores.shape
    width = past + LANES
    nblk = width // LANES
    return pl.pallas_call(
        functools.partial(_bias_sample_kernel, k_top), grid=(nblk,),
        in_specs=[pl.BlockSpec((db, past), lambda i: (0, 0)), pl.BlockSpec((db, LANES), lambda i: (0, 0))],
        out_specs=pl.BlockSpec((1, db, LANES * N_HEADS), lambda i: (i, 0, 0)),
        out_shape=jax.ShapeDtypeStruct((nblk, db, LANES * N_HEADS), F32),
        scratch_shapes=[pltpu.VMEM((db, width), I32), pltpu.VMEM((db, LANES), I32)],
        compiler_params=_cparams("arbitrary"), name="bias_sample",
    )(scores, score_new)


def _head_mask(n_heads, head_dim):
    width = n_heads * head_dim
    h = lax.broadcasted_iota(I32, (8, width), 0)
    l = lax.broadcasted_iota(I32, (8, width), 1)
    return (l // head_dim) == h


PAGE_ROWS = PAGE_SIZE * N_HEADS


def _lanes_to_column(x):
    sub = lax.broadcasted_iota(I32, x.shape, 0)
    lane = lax.broadcasted_iota(I32, x.shape, 1)
    return jnp.sum(jnp.where(sub == lane, x, 0.0), axis=1, keepdims=True)


def _attn_sample_kernel(pt_ref, q_ref, kn_ref, vn_ref, *refs):
    nb = ATT_PAGES
    bias = refs[:nb]
    biasn_ref = refs[nb]
    kp = refs[nb + 1:2 * nb + 1]
    vp = refs[2 * nb + 1:3 * nb + 1]
    o_ref = refs[3 * nb + 1]
    m_ref, l_ref, acc_ref = refs[3 * nb + 2:]
    s_id = pl.program_id(1)
    nt = (((1,), (1,)), ((), ()))
    q = q_ref[0]
    sub = lax.broadcasted_iota(I32, (N_HEADS, PAGE_ROWS), 0)
    col = lax.broadcasted_iota(I32, (N_HEADS, PAGE_ROWS), 1)
    own = (col % N_HEADS) == sub
    n_tiles = PAGE_ROWS // LANES

    def fold(x, op):
        out = x[:, 0:LANES]
        for t in range(1, n_tiles):
            out = op(out, x[:, t * LANES:(t + 1) * LANES])
        return out

    def across_keys(x, op):
        for sh in (8, 16, 32, 64):
            x = op(x, pltpu.roll(x, sh, 1))
        return x

    @pl.when(s_id == 0)
    def _():
        m_ref[...] = jnp.full(m_ref.shape, MASK_VALUE, F32)
        l_ref[...] = jnp.zeros(l_ref.shape, F32)
        acc_ref[...] = jnp.zeros(acc_ref.shape, F32)

    rows = []
    for j in range(nb):
        st = lax.dot_general(q, kp[j][0].astype(BF16), nt, preferred_element_type=F32)
        rows.append(jnp.sum(jnp.where(own, st, 0.0), axis=0, keepdims=True) + bias[j][0, 0])
    s = jnp.concatenate(rows, axis=0)
    blk_max = fold(jnp.max(s, axis=0, keepdims=True), jnp.maximum)
    blk_max = across_keys(jnp.broadcast_to(blk_max, (N_HEADS, LANES)), jnp.maximum)
    m_old = m_ref[...]
    m_new = jnp.maximum(m_old, blk_max)
    alpha = jnp.exp(m_old - m_new)
    m_ref[...] = m_new
    e = jnp.exp(s - jnp.concatenate([m_new[0:1]] * n_tiles, axis=1))
    l_ref[...] = alpha * l_ref[...] + fold(jnp.sum(e, axis=0, keepdims=True), jnp.add)
    pv = jnp.zeros(acc_ref.shape, F32)
    for j in range(nb):
        pm = jnp.where(own, jnp.broadcast_to(e[j:j + 1], own.shape), 0.0).astype(BF16)
        pv = pv + jnp.dot(pm, vp[j][0].astype(BF16), preferred_element_type=F32)
    acc_ref[...] = _lanes_to_column(alpha) * acc_ref[...] + pv

    @pl.when(s_id == pl.num_programs(1) - 1)
    def _():
        m_col = _lanes_to_column(m_ref[...])
        l_col = _lanes_to_column(across_keys(l_ref[...], jnp.add))
        kn = kn_ref[0].astype(BF16).astype(F32)
        s_new = jnp.sum(q.astype(F32) * kn, axis=1, keepdims=True) + biasn_ref[0, 0, :, 0:1]
        m_fin = jnp.maximum(m_col, s_new)
        a = jnp.exp(m_col - m_fin)
        e_new = jnp.exp(s_new - m_fin)
        vn = vn_ref[0].astype(BF16).astype(F32)
        acc = a * acc_ref[...] + e_new.astype(BF16).astype(F32) * vn
        o_ref[0] = (acc / (a * l_col + e_new)).astype(BF16)


def _attn_sample(page_table, q_s, k_new, v_new, bias, cache_k, cache_v):
    db, n_pages = page_table.shape
    nj = n_pages // ATT_PAGES

    def page_spec(j):
        return pl.BlockSpec((1, PAGE_ROWS, HEAD_DIM),
                            lambda b, s, pt, j=j: (pt[b, s * ATT_PAGES + j], 0, 0))

    def bias_spec(j):
        return pl.BlockSpec((1, 1, 1, PAGE_ROWS), lambda b, s, pt, j=j: (s * ATT_PAGES + j, b, 0, 0))

    tok = lambda b, s, pt: (b, 0, 0)
    head_blk = pl.BlockSpec((1, N_HEADS, HEAD_DIM), tok)
    grid_spec = pltpu.PrefetchScalarGridSpec(
        num_scalar_prefetch=1, grid=(db, nj),
        in_specs=[head_blk, head_blk, head_blk]
                 + [bias_spec(j) for j in range(ATT_PAGES)]
                 + [pl.BlockSpec((1, 1, 1, PAGE_ROWS), lambda b, s, pt: (n_pages, b, 0, 0))]
                 + [page_spec(j) for j in range(ATT_PAGES)] * 2,
        out_specs=head_blk,
        scratch_shapes=[pltpu.VMEM((N_HEADS, LANES), F32), pltpu.VMEM((N_HEADS, LANES), F32),
                        pltpu.VMEM((N_HEADS, HEAD_DIM), F32)])
    return pl.pallas_call(
        _attn_sample_kernel, grid_spec=grid_spec,
        out_shape=jax.ShapeDtypeStruct((db, N_HEADS, HEAD_DIM), BF16),
        compiler_params=_cparams("arbitrary", "arbitrary"), name="attn_sample",
    )(page_table, q_s, k_new, v_new, *([bias] * (ATT_PAGES + 1)),
      *([cache_k] * ATT_PAGES), *([cache_v] * ATT_PAGES))


def _outproj_kernel(x_ref, att_ref, pool_ref, wo_ref, g_ref, wq_ref, x1_ref, q_ref):
    x1 = x_ref[...] + jnp.dot(att_ref[...], wo_ref[0:ATT_WIDTH, :], preferred_element_type=F32) \
        + jnp.dot(pool_ref[...], wo_ref[ATT_WIDTH:, :], preferred_element_type=F32)
    x1_ref[...] = x1
    h = _rms(x1, g_ref[...]).astype(BF16)
    q_ref[...] = jnp.dot(h, wq_ref[...], preferred_element_type=F32).astype(BF16)


def _outproj(x, att, pool, wo, g, wq, tm, row0=0):
    n, d = att.shape[0], x.shape[1]
    t0 = row0 // tm
    row = lambda i: (i, 0)
    off = lambda i: (t0 + i, 0)
    full = lambda i: (0, 0)
    return pl.pallas_call(
        _outproj_kernel, grid=(n // tm,),
        in_specs=[pl.BlockSpec((tm, d), off), pl.BlockSpec((tm, ATT_WIDTH), row),
                  pl.BlockSpec((tm, POOL_WIDTH), off), pl.BlockSpec(wo.shape, full),
                  pl.BlockSpec((1, d), full), pl.BlockSpec(wq.shape, full)],
        out_specs=(pl.BlockSpec((tm, d), row), pl.BlockSpec((tm, MEM_WIDTH), row)),
        out_shape=(jax.ShapeDtypeStruct((n, d), F32), jax.ShapeDtypeStruct((n, MEM_WIDTH), BF16)),
        compiler_params=_cparams("parallel"), name="outproj",
    )(x, att, pool, wo, g, wq)


def _memkv_kernel(mem_ref, g_ref, wk_ref, wv_ref, kf_ref, vf_ref, kb_ref, vb_ref):
    m = _rms(mem_ref[...], g_ref[...]).astype(BF16)
    k = jnp.dot(m, wk_ref[...], preferred_element_type=F32)
    v = jnp.dot(m, wv_ref[...], preferred_element_type=F32)
    kf_ref[...] = k
    vf_ref[...] = v
    kb_ref[...] = k.astype(BF16)
    vb_ref[...] = v.astype(BF16)


def _memkv(mem, g, wk, wv, tm):
    n, d = mem.shape
    row = lambda i: (i, 0)
    full = lambda i: (0, 0)
    spec = pl.BlockSpec((tm, MEM_WIDTH), row)
    return pl.pallas_call(
        _memkv_kernel, grid=(n // tm,),
        in_specs=[pl.BlockSpec((tm, d), row), pl.BlockSpec((1, d), full),
                  pl.BlockSpec(wk.shape, full), pl.BlockSpec(wv.shape, full)],
        out_specs=(spec, spec, spec, spec),
        out_shape=(jax.ShapeDtypeStruct((n, MEM_WIDTH), F32), jax.ShapeDtypeStruct((n, MEM_WIDTH), F32),
                   jax.ShapeDtypeStruct((n, MEM_WIDTH), BF16), jax.ShapeDtypeStruct((n, MEM_WIDTH), BF16)),
        compiler_params=_cparams("parallel"), name="memkv",
    )(mem, g, wk, wv)


def _cross_prompt_kernel(q_ref, mk_ref, mv_ref, o_ref):
    nt = (((1,), (1,)), ((), ()))
    for h in range(N_MEM_HEADS):
        sl = slice(h * MEM_HEAD_DIM, (h + 1) * MEM_HEAD_DIM)
        s = lax.dot_general(q_ref[:, sl], mk_ref[:, sl], nt, preferred_element_type=F32)
        s = s * (MEM_HEAD_DIM ** -0.5)
        e = jnp.exp(s - jnp.max(s, axis=1, keepdims=True))
        p = e / jnp.sum(e, axis=1, keepdims=True)
        o_ref[:, sl] = jnp.dot(p.astype(BF16), mv_ref[:, sl], preferred_element_type=F32).astype(BF16)


def _cross_prompt(q, mk, mv, b0, batch, n_mem, tm):
    n = q.shape[0]
    nt = n // batch // tm
    blk = lambda b, i: (b * nt + i, 0)
    per_batch = lambda b, i: (b0 + b, 0)
    return pl.pallas_call(
        _cross_prompt_kernel, grid=(batch, nt),
        in_specs=[pl.BlockSpec((tm, MEM_WIDTH), blk), pl.BlockSpec((n_mem, MEM_WIDTH), per_batch),
                  pl.BlockSpec((n_mem, MEM_WIDTH), per_batch)],
        out_specs=pl.BlockSpec((tm, MEM_WIDTH), blk),
        out_shape=jax.ShapeDtypeStruct((n, MEM_WIDTH), BF16),
        compiler_params=_cparams("parallel", "parallel"), name="cross_prompt",
    )(q, mk, mv)


def _cross_sample_kernel(q_ref, mk_ref, mv_ref, o_ref):
    hm = _head_mask(N_MEM_HEADS, MEM_HEAD_DIM)
    qbd = jnp.where(hm, jnp.broadcast_to(q_ref[0].astype(F32), hm.shape), 0.0).astype(BF16)
    nt = (((1,), (1,)), ((), ()))
    s = lax.dot_general(qbd, mk_ref[0].astype(BF16), nt, preferred_element_type=F32)
    s = s * (MEM_HEAD_DIM ** -0.5)
    e = jnp.exp(s - jnp.max(s, axis=1, keepdims=True))
    p = e / jnp.sum(e, axis=1, keepdims=True)
    o = jnp.dot(p.astype(BF16), mv_ref[0].astype(BF16), preferred_element_type=F32)
    o_ref[0] = jnp.sum(jnp.where(hm, o, 0.0), axis=0, keepdims=True).astype(BF16)


def _cross_sample(q, mem_k, mem_v):
    db, n_mem, _ = mem_k.shape
    tok = lambda b: (b, 0, 0)
    return pl.pallas_call(
        _cross_sample_kernel, grid=(db,),
        in_specs=[pl.BlockSpec((1, 1, MEM_WIDTH), tok), pl.BlockSpec((1, n_mem, MEM_WIDTH), tok),
                  pl.BlockSpec((1, n_mem, MEM_WIDTH), tok)],
        out_specs=pl.BlockSpec((1, 1, MEM_WIDTH), tok),
        out_shape=jax.ShapeDtypeStruct((db, 1, MEM_WIDTH), BF16),
        compiler_params=_cparams("parallel"), name="cross_sample",
    )(q.reshape(db, 1, MEM_WIDTH), mem_k, mem_v)


def _top_rows(s, k):
    rows = s.shape[0]
    row = lax.broadcasted_iota(I32, s.shape, 0)
    vals, idxs = [], []
    for _ in range(k):
        m = jnp.max(s, axis=0, keepdims=True)
        am = jnp.min(jnp.where(s == m, row, rows), axis=0, keepdims=True)
        vals.append(m)
        idxs.append(am)
        s = jnp.where(row == am, -jnp.inf, s)
    return jnp.concatenate(vals, axis=0), jnp.concatenate(idxs, axis=0)


_CAND_ROWS = PEER_TOPK + 8 * 7 + 8


def _route_kernel(x1_ref, o_ref, wmo_ref, g_ref, wpq_ref, sk_ref,
                  x2_ref, h_ref, eid_ref, gate_ref, pq_ref, ts_ref, ti_ref):
    tm = x1_ref.shape[0]
    x2 = x1_ref[...] + jnp.dot(o_ref[...], wmo_ref[...], preferred_element_type=F32)
    x2_ref[...] = x2
    hf = _rms(x2, g_ref[...])
    h_ref[...] = hf
    hb = hf.astype(BF16)
    for hp in range(2 * PEER_HEADS):
        cols = slice(hp * PEER_HALF_DIM, (hp + 1) * PEER_HALF_DIM)
        pq_ref[hp] = jnp.dot(hb, wpq_ref[:, cols], preferred_element_type=F32).astype(BF16)
    nt = (((1,), (1,)), ((), ()))

    def half_topk(hp, carry):
        st = lax.dot_general(sk_ref[hp], pq_ref[hp], nt, preferred_element_type=F32)
        vals, idxs = _top_rows(st, PEER_TOPK)
        ts_ref[hp] = vals
        ti_ref[hp] = idxs
        return carry

    lax.fori_loop(0, 2 * PEER_HEADS, half_topk, 0)

    grp = lax.broadcasted_iota(I32, (8, tm), 0)

    def head_select(h, carry):
        s1, s2 = ts_ref[2 * h], ts_ref[2 * h + 1]
        i1, i2 = ti_ref[2 * h], ti_ref[2 * h + 1]
        cs = [s1[0:1] + s2]
        ce = [i1[0:1] * PEER_N_KEYS + i2]
        for i in range(1, 8):
            valid = grp < (PEER_TOPK // (i + 1))
            cs.append(jnp.where(valid, s1[i:i + 1] + s2[0:8], -jnp.inf))
            ce.append(i1[i:i + 1] * PEER_N_KEYS + i2[0:8])
        cs.append(s1[8:16] + s2[0:1])
        ce.append(i1[8:16] * PEER_N_KEYS + i2[0:1])
        cand_s = jnp.concatenate(cs, axis=0)
        cand_e = jnp.concatenate(ce, axis=0)
        row = lax.broadcasted_iota(I32, cand_s.shape, 0)
        top_s, top_e = [], []
        for _ in range(PEER_TOPK):
            m = jnp.max(cand_s, axis=0, keepdims=True)
            am = jnp.min(jnp.where(cand_s == m, row, _CAND_ROWS), axis=0, keepdims=True)
            hit = row == am
            top_s.append(m)
            top_e.append(jnp.max(jnp.where(hit, cand_e, -1), axis=0, keepdims=True))
            cand_s = jnp.where(hit, -jnp.inf, cand_s)
        top_s = jnp.concatenate(top_s, axis=0)
        e = jnp.exp(top_s - top_s[0:1])
        gate = e / jnp.sum(e, axis=0, keepdims=True)
        r0 = pl.multiple_of(h * PEER_TOPK, PEER_TOPK)
        eid_ref[pl.ds(r0, PEER_TOPK), :] = jnp.concatenate(top_e, axis=0)
        gate_ref[pl.ds(r0, PEER_TOPK), :] = gate
        return carry

    lax.fori_loop(0, PEER_HEADS, head_select, 0)


def _route(x1, o, wmo, g, wpq, sk, tm):
    n, d = x1.shape
    row = lambda i: (i, 0)
    full = lambda i: (0, 0)
    col = lambda i: (0, i)
    return pl.pallas_call(
        _route_kernel, grid=(n // tm,),
        in_specs=[pl.BlockSpec((tm, d), row), pl.BlockSpec((tm, MEM_WIDTH), row),
                  pl.BlockSpec(wmo.shape, full), pl.BlockSpec((1, d), full),
                  pl.BlockSpec(wpq.shape, full), pl.BlockSpec(sk.shape, lambda i: (0, 0, 0))],
        out_specs=(pl.BlockSpec((tm, d), row), pl.BlockSpec((tm, d), row),
                   pl.BlockSpec((N_SEL, tm), col), pl.BlockSpec((N_SEL, tm), col)),
        out_shape=(jax.ShapeDtypeStruct((n, d), F32), jax.ShapeDtypeStruct((n, d), F32),
                   jax.ShapeDtypeStruct((N_SEL, n), I32), jax.ShapeDtypeStruct((N_SEL, n), F32)),
        scratch_shapes=[pltpu.VMEM((2 * PEER_HEADS, tm, PEER_HALF_DIM), BF16),
                        pltpu.VMEM((2 * PEER_HEADS, PEER_TOPK, tm), F32),
                        pltpu.VMEM((2 * PEER_HEADS, PEER_TOPK, tm), I32)],
        compiler_params=_cparams("parallel"), name="route",
    )(x1, o, wmo, g, wpq, sk)


SC_CORES = 2
SC_SUBCORES = 16
SC_LANES = 16
SC_CHUNK = SC_LANES
SC_NBUF = 4
SC_UNROLL = 2
SC_TOKENS = 16


def _sc_peer(kind, lhs, eid, table, tb):
    n = lhs.shape[0]
    d = table.shape[1]
    n_workers = SC_CORES * SC_SUBCORES
    per_w = n // n_workers
    assert per_w * n_workers == n and per_w % tb == 0
    n_blocks = per_w // tb
    n_chunks = N_SEL // SC_CHUNK
    items = tb * n_chunks
    assert items % SC_NBUF == 0
    nj = d // SC_LANES
    is_act = kind == "act"
    res_w = N_SEL if is_act else d
    mesh = plsc.VectorSubcoreMesh(core_axis_name="c", subcore_axis_name="s")

    @functools.partial(
        pl.kernel, mesh=mesh, compiler_params=pltpu.CompilerParams(needs_layout_passes=False),
        out_type=jax.ShapeDtypeStruct((n, res_w), F32),
        scratch_types=[pltpu.VMEM((tb, lhs.shape[1]), F32), pltpu.VMEM((tb, N_SEL), I32),
                       pltpu.VMEM((tb, res_w), F32)]
                      + [pltpu.VMEM((SC_CHUNK, d), F32)] * SC_NBUF + [pltpu.SemaphoreType.DMA] * SC_NBUF)
    def sc_kernel(lhs_hbm, eid_hbm, tab_hbm, res_hbm, lhs_v, idx_v, res_v, *rest):
        bufs, sems = rest[:SC_NBUF], rest[SC_NBUF:]
        wid = lax.axis_index("s") * SC_CORES + lax.axis_index("c")
        lane = lax.iota(I32, SC_LANES)

        def gather(item, b):
            t = item // n_chunks
            c = item % n_chunks
            idx = idx_v[t, pl.ds(c * SC_CHUNK, SC_CHUNK)]
            return pltpu.make_async_copy(tab_hbm.at[idx], bufs[b], sems[b])

        @pl.loop(0, n_blocks)
        def _(blk):
            tok0 = wid * per_w + blk * tb
            pltpu.sync_copy(lhs_hbm.at[pl.ds(tok0, tb)], lhs_v)
            pltpu.sync_copy(eid_hbm.at[pl.ds(tok0, tb)], idx_v)
            for b in range(SC_NBUF):
                gather(b, b).start()

            if not is_act:
                @pl.loop(0, tb)
                def _(t):
                    @pl.loop(0, nj)
                    def _(j):
                        res_v[t, pl.ds(j * SC_LANES, SC_LANES)] = jnp.zeros((SC_LANES,), F32)

            @pl.loop(0, items, step=SC_NBUF)
            def _(it):
                for b in range(SC_NBUF):
                    item = it + b
                    t = item // n_chunks
                    c = item % n_chunks
                    gather(item, b).wait()
                    if is_act:
                        def dot_step(j, accs):
                            sl = pl.ds(j * SC_LANES, SC_LANES)
                            xj = lhs_v[t, sl]
                            return tuple(accs[r] + bufs[b][r, sl] * xj for r in range(SC_CHUNK))

                        init = tuple(jnp.zeros((SC_LANES,), F32) for _ in range(SC_CHUNK))
                        accs = plsc.parallel_loop(0, nj, 1, unroll=SC_UNROLL, carry=init)(dot_step)
                        res = jnp.zeros((SC_LANES,), F32)
                        for r in range(SC_CHUNK):
                            res = jnp.where(lane == r, jnp.sum(accs[r]), res)
                        res_v[t, pl.ds(c * SC_CHUNK, SC_CHUNK)] = res
                    else:
                        cvec = lhs_v[t, pl.ds(c * SC_CHUNK, SC_CHUNK)]
                        splat = [jnp.broadcast_to(jnp.sum(jnp.where(lane == r, cvec, 0.0)), (SC_LANES,))
                                 for r in range(SC_CHUNK)]

                        @plsc.parallel_loop(0, nj, 1, unroll=SC_UNROLL)
                        def _(j):
                            sl = pl.ds(j * SC_LANES, SC_LANES)
                            o = res_v[t, sl]
                            for r in range(SC_CHUNK):
                                o = o + splat[r] * bufs[b][r, sl]
                            res_v[t, sl] = o

                    @pl.when(item + SC_NBUF < items)
                    def _():
                        gather(item + SC_NBUF, b).start()

            pltpu.sync_copy(res_v, res_hbm.at[pl.ds(tok0, tb)])

    return sc_kernel(lhs, eid, table)


def _coef_kernel(act_ref, gate_ref, o_ref):
    act = act_ref[...]
    o_ref[...] = gate_ref[...] * (0.5 * act * (1.0 + lax.erf(act * (2.0 ** -0.5))))


def _coef(act, gate, tm):
    n = act.shape[0]
    spec = pl.BlockSpec((tm, N_SEL), lambda i: (i, 0))
    return pl.pallas_call(
        _coef_kernel, grid=(n // tm,), in_specs=[spec, spec], out_specs=spec,
        out_shape=jax.ShapeDtypeStruct((n, N_SEL), F32),
        compiler_params=_cparams("parallel"), name="peer_coef",
    )(act, gate)


def _final_kernel(x2_ref, o_ref, g_ref, y_ref):
    y_ref[...] = _rms(x2_ref[...] + o_ref[...], g_ref[...])


def _final(x2, out, g, tm):
    n, d = x2.shape
    spec = pl.BlockSpec((tm, d), lambda i: (i, 0))
    return pl.pallas_call(
        _final_kernel, grid=(n // tm,),
        in_specs=[spec, spec, pl.BlockSpec((1, d), lambda i: (0, 0))], out_specs=spec,
        out_shape=jax.ShapeDtypeStruct((n, d), F32),
        compiler_params=_cparams("parallel"), name="final_norm",
    )(x2, out, g)


def _peer_ffn_final(x2, h, eid_t, gate_t, peer_u, peer_v, g_final):
    n = x2.shape[0]
    per_w = n // (SC_CORES * SC_SUBCORES)
    tb = SC_TOKENS if per_w % SC_TOKENS == 0 else per_w
    tm = _row_tile(n, 512)
    eid = eid_t.T
    act = _sc_peer("act", h, eid, peer_u, tb)
    coef = _coef(act, gate_t.T, tm)
    out = _sc_peer("out", coef, eid, peer_v, tb)
    return _final(x2, out, g_final, tm)


def _rope_tables(first_pos, t):
    half = HEAD_DIM // 2
    inv_freq = ROPE_THETA ** (-jnp.arange(half, dtype=F32) / half)
    pos = first_pos + jnp.arange(t, dtype=I32)
    ang = pos.astype(F32)[:, None] * inv_freq[None, :]
    cos, sin = jnp.cos(ang), jnp.sin(ang)
    cos_t = jnp.concatenate([cos, cos, cos, cos], axis=1)
    sin_t = jnp.concatenate([-sin, sin, -sin, sin], axis=1)
    return cos_t, sin_t


def _row_tile(n, pref):
    return pref if n % pref == 0 else n


def kernel(x_prompt, x_sample, mem_prompt, cache_k, cache_v, cache_idx_k, state_pool, cache_mem_k,
           cache_mem_v, page_table, norm_mix, w_in, pool_w, pool_scale, w_out, norm_cross, norm_mem,
           w_mq, w_mk, w_mv, w_mo, norm_ffn, w_pq, sub_keys, peer_u, peer_v, norm_final):
    batch, seq, d = x_prompt.shape
    db, ds, _ = x_sample.shape
    depth = w_in.shape[0]
    assert depth == 1 and ds == 1
    n_pages = page_table.shape[1]
    past = n_pages * PAGE_SIZE
    n_mem = mem_prompt.shape[1]
    n_p, n_s = batch * seq, db * ds
    l = 0

    w = w_in[l]
    c_ki = 4 * ATT_WIDTH
    c_wi = c_ki + IDX_HEAD_DIM
    c_p = c_wi + N_IDX_HEADS
    wbig = jnp.concatenate([w[:, :c_ki], w[:, c_p:]], axis=1).astype(BF16)
    wsm = jnp.pad(w[:, c_ki:c_p], ((0, 0), (0, LANES - (c_p - c_ki)))).astype(BF16)
    g_mix = norm_mix[l][None, :]
    pw = pool_w[l].astype(BF16)
    ps = pool_scale[l][None, :]
    wo = w_out[l].astype(BF16)
    g_cross = norm_cross[l][None, :]
    wq = w_mq[l].astype(BF16)
    wmo = w_mo[l].astype(BF16)
    g_ffn = norm_ffn[l][None, :]
    wpq = w_pq[l].astype(BF16)
    sk = sub_keys[l].reshape(2 * PEER_HEADS, PEER_N_KEYS, PEER_HALF_DIM).astype(BF16)
    g_final = norm_final[None, :]
    pu, pv = peer_u[l], peer_v[l]

    tm = _row_tile(seq, 512)
    cos_p, sin_p = _rope_tables(0, seq)
    xp = x_prompt.reshape(n_p, d)
    q, kf, kb, vf, vb, qi, kif, kib, wi, p = _inproj(xp, g_mix, wbig, wsm, cos_p, sin_p, tm)
    pool_out = _pool_prompt(p, pw, ps, batch, tm)
    mkf, mvf, mkb, mvb = _memkv(mem_prompt.reshape(batch * n_mem, d), norm_mem[l][None, :],
                                w_mk[l].astype(BF16), w_mv[l].astype(BF16), _row_tile(batch * n_mem, 256))
    ys = []
    for b in range(batch):
        att = _dsa_prompt(q, qi, wi, kb, vb, kib, b, 1, seq)
        x1, qm = _outproj(xp, att, pool_out, wo, g_cross, wq, tm, row0=b * seq)
        o = _cross_prompt(qm, mkb, mvb, b, 1, n_mem, tm)
        x2, h, eid_t, gate_t = _route(x1, o, wmo, g_ffn, wpq, sk, _row_tile(seq, 256))
        ys.append(_peer_ffn_final(x2, h, eid_t, gate_t, pu, pv, g_final))
    y_prompt = jnp.stack(ys, axis=0)

    new_k_prompt = kf.reshape(1, batch, seq, N_HEADS, HEAD_DIM)
    new_v_prompt = vf.reshape(1, batch, seq, N_HEADS, HEAD_DIM)
    new_idx_k_prompt = kif.reshape(1, batch, seq, IDX_HEAD_DIM)
    new_pool_prompt = p.reshape(batch, seq, POOL_WIDTH)[None, :, seq - POOL_STATE_LEN:, :]
    new_mem_k_prompt = mkf.reshape(1, batch, n_mem, N_MEM_HEADS, MEM_HEAD_DIM)
    new_mem_v_prompt = mvf.reshape(1, batch, n_mem, N_MEM_HEADS, MEM_HEAD_DIM)

    cos_s, sin_s = _rope_tables(past, 1)
    cos_s = jnp.broadcast_to(cos_s, (n_s, LANES))
    sin_s = jnp.broadcast_to(sin_s, (n_s, LANES))
    xs = x_sample.reshape(n_s, d)
    q, kf, kb, vf, vb, qi, kif, kib, wi, p = _inproj(xs, g_mix, wbig, wsm, cos_s, sin_s, n_s)
    n_pool = cache_k.shape[1]
    ck = cache_k[l].reshape(n_pool, PAGE_ROWS, HEAD_DIM)
    cv = cache_v[l].reshape(n_pool, PAGE_ROWS, HEAD_DIM)
    qi_s = jnp.transpose(qi, (1, 0, 2))
    scores, score_new = _idx_sample(page_table, qi_s, wi.reshape(n_s, N_IDX_HEADS, 1),
                                    kib.reshape(n_s, 1, IDX_HEAD_DIM), cache_idx_k[l])
    k_top = min(TOPK_MAX, (past + ds) // 4)
    bias = _bias_sample(scores.reshape(n_s, past), score_new.reshape(n_s, LANES), k_top)
    heads = (n_s, N_HEADS, HEAD_DIM)
    att = _attn_sample(page_table, q.reshape(heads), kf.reshape(heads), vf.reshape(heads),
                       bias.reshape(bias.shape[0], n_s, 1, PAGE_ROWS), ck, cv).reshape(n_s, ATT_WIDTH)
    state_t = jnp.transpose(state_pool[l], (1, 0, 2))
    pool_out = _pool_sample(state_t, p, pw, ps)
    x1, qm = _outproj(xs, att, pool_out, wo, g_cross, wq, n_s)
    o = _cross_sample(qm, cache_mem_k[l].reshape(db, n_mem, MEM_WIDTH),
                      cache_mem_v[l].reshape(db, n_mem, MEM_WIDTH)).reshape(n_s, MEM_WIDTH)
    x2, h, eid_t, gate_t = _route(x1, o, wmo, g_ffn, wpq, sk, n_s)
    y_sample = _peer_ffn_final(x2, h, eid_t, gate_t, pu, pv, g_final).reshape(db, ds, d)

    new_k_sample = kf.reshape(1, db, ds, N_HEADS, HEAD_DIM)
    new_v_sample = vf.reshape(1, db, ds, N_HEADS, HEAD_DIM)
    new_idx_k_sample = kif.reshape(1, db, ds, IDX_HEAD_DIM)
    new_pool_sample = jnp.concatenate([state_pool[l][:, 1:, :], p[:, None, :]], axis=1)[None]

    return (y_prompt, y_sample, new_k_prompt, new_v_prompt, new_idx_k_prompt, new_pool_prompt,
            new_mem_k_prompt, new_mem_v_prompt, new_k_sample, new_v_sample, new_idx_k_sample,
            new_pool_sample)
```

```python
import functools

import numpy as np
import jax
import jax.numpy as jnp
from jax import lax
from jax.experimental import pallas as pl
from jax.experimental.pallas import tpu as pltpu
from jax.experimental.pallas import tpu_sc as plsc

F32 = jnp.float32
BF16 = jnp.bfloat16
I32 = jnp.int32

N_HEADS = 8
HEAD_DIM = 64
ATT_WIDTH = N_HEADS * HEAD_DIM
N_IDX_HEADS = 8
IDX_HEAD_DIM = 64
TOPK_MAX = 256
POOL_WINDOWS = (2, 4, 8, 16)
POOL_GROUP_DIM = 128
POOL_WIDTH = POOL_GROUP_DIM * len(POOL_WINDOWS)
POOL_STATE_LEN = max(POOL_WINDOWS) - 1
N_MEM_HEADS = 4
MEM_HEAD_DIM = 128
MEM_WIDTH = N_MEM_HEADS * MEM_HEAD_DIM
PEER_HEADS = 8
PEER_N_KEYS = 128
PEER_HALF_DIM = 128
PEER_TOPK = 16
N_SEL = PEER_HEADS * PEER_TOPK
PAGE_SIZE = 128
ROPE_THETA = 10000.0
RMS_EPS = 1e-6

LANES = 128
MASK_VALUE = -1e30
INT_MIN = np.int32(-2 ** 31)
NEG_INF_KEY = np.int32(np.uint32(0x807FFFFF).astype(np.int64) - 2 ** 32)

VMEM_LIMIT = 56 * 1024 * 1024


def _cparams(*sem):
    return pltpu.CompilerParams(dimension_semantics=sem, vmem_limit_bytes=VMEM_LIMIT)


def _rms(x, g):
    ms = jnp.mean(x * x, axis=-1, keepdims=True)
    return x * lax.rsqrt(ms + RMS_EPS) * g


def _inproj_kernel(x_ref, g_ref, wbig_ref, wsm_ref, cos_ref, sin_ref,
                   q_ref, kf_ref, kb_ref, vf_ref, vb_ref, qi_ref, kif_ref, kib_ref, wi_ref, p_ref):
    h = _rms(x_ref[...], g_ref[...]).astype(BF16)
    cos = cos_ref[...]
    sin = sin_ref[...]
    lane = lax.broadcasted_iota(I32, (1, LANES), 1)
    first_half = (lane % HEAD_DIM) < (HEAD_DIM // 2)

    def rope(z):
        partner = jnp.where(first_half, pltpu.roll(z, LANES - HEAD_DIM // 2, 1),
                            pltpu.roll(z, HEAD_DIM // 2, 1))
        return z * cos + partner * sin

    def proj(c0, width):
        return jnp.dot(h, wbig_ref[:, c0:c0 + width], preferred_element_type=F32)

    for s in range(ATT_WIDTH // LANES):
        sl = slice(s * LANES, (s + 1) * LANES)
        zq = rope(proj(s * LANES, LANES))
        q_ref[:, sl] = (zq * (HEAD_DIM ** -0.5)).astype(BF16)
        zk = rope(proj(ATT_WIDTH + s * LANES, LANES))
        kf_ref[:, sl] = zk
        kb_ref[:, sl] = zk.astype(BF16)
        zqi = rope(proj(3 * ATT_WIDTH + s * LANES, LANES)).astype(BF16)
        qi_ref[2 * s] = zqi[:, :IDX_HEAD_DIM]
        qi_ref[2 * s + 1] = zqi[:, IDX_HEAD_DIM:]
    zv = proj(2 * ATT_WIDTH, ATT_WIDTH)
    vf_ref[...] = zv
    vb_ref[...] = zv.astype(BF16)
    p_ref[...] = proj(4 * ATT_WIDTH, POOL_WIDTH)
    zs = jnp.dot(h, wsm_ref[...], preferred_element_type=F32)
    zki = rope(zs)[:, :IDX_HEAD_DIM]
    kif_ref[...] = zki
    kib_ref[...] = zki.astype(BF16)
    wi_ref[...] = zs[:, IDX_HEAD_DIM:IDX_HEAD_DIM + N_IDX_HEADS] * (
        (N_IDX_HEADS ** -0.5) * (IDX_HEAD_DIM ** -0.5))


def _inproj(x, g, wbig, wsm, cos, sin, tm):
    n, d = x.shape
    nt = n // tm
    nrt = cos.shape[0] // tm
    row = lambda i: (i, 0)
    full = lambda i: (0, 0)
    out_shape = (
        jax.ShapeDtypeStruct((n, ATT_WIDTH), BF16),
        jax.ShapeDtypeStruct((n, ATT_WIDTH), F32),
        jax.ShapeDtypeStruct((n, ATT_WIDTH), BF16),
        jax.ShapeDtypeStruct((n, ATT_WIDTH), F32),
        jax.ShapeDtypeStruct((n, ATT_WIDTH), BF16),
        jax.ShapeDtypeStruct((N_IDX_HEADS, n, IDX_HEAD_DIM), BF16),
        jax.ShapeDtypeStruct((n, IDX_HEAD_DIM), F32),
        jax.ShapeDtypeStruct((n, IDX_HEAD_DIM), BF16),
        jax.ShapeDtypeStruct((n, N_IDX_HEADS), F32),
        jax.ShapeDtypeStruct((n, POOL_WIDTH), F32),
    )
    out_specs = (
        pl.BlockSpec((tm, ATT_WIDTH), row), pl.BlockSpec((tm, ATT_WIDTH), row),
        pl.BlockSpec((tm, ATT_WIDTH), row), pl.BlockSpec((tm, ATT_WIDTH), row),
        pl.BlockSpec((tm, ATT_WIDTH), row),
        pl.BlockSpec((N_IDX_HEADS, tm, IDX_HEAD_DIM), lambda i: (0, i, 0)),
        pl.BlockSpec((tm, IDX_HEAD_DIM), row), pl.BlockSpec((tm, IDX_HEAD_DIM), row),
        pl.BlockSpec((tm, N_IDX_HEADS), row), pl.BlockSpec((tm, POOL_WIDTH), row),
    )
    return pl.pallas_call(
        _inproj_kernel, grid=(nt,),
        in_specs=[pl.BlockSpec((tm, d), row), pl.BlockSpec((1, d), full),
                  pl.BlockSpec(wbig.shape, full), pl.BlockSpec(wsm.shape, full),
                  pl.BlockSpec((tm, LANES), lambda i: (i % nrt, 0)),
                  pl.BlockSpec((tm, LANES), lambda i: (i % nrt, 0))],
        out_specs=out_specs, out_shape=out_shape,
        compiler_params=_cparams("parallel"), name="inproj",
    )(x, g, wbig, wsm, cos, sin)


HALO = 16


def _pool_prompt_kernel(p_ref, pw_ref, ps_ref, o_ref, ext_ref):
    tm = p_ref.shape[0]
    i = pl.program_id(1)

    @pl.when(i == 0)
    def _():
        ext_ref[0:HALO, :] = jnp.zeros((HALO, POOL_WIDTH), F32)

    @pl.when(i > 0)
    def _():
        ext_ref[0:HALO, :] = ext_ref[tm:tm + HALO, :]

    ext_ref[HALO:HALO + tm, :] = p_ref[...]
    pos = i * tm + lax.broadcasted_iota(I32, (tm, 1), 0)
    for g, w in enumerate(POOL_WINDOWS):
        sl = slice(g * POOL_GROUP_DIM, (g + 1) * POOL_GROUP_DIM)
        cur = ext_ref[HALO:HALO + tm, sl]
        s = cur
        for j in range(1, w):
            s = s + ext_ref[HALO - j:HALO - j + tm, sl]
        cnt = jnp.minimum(w, pos + 1).astype(F32)
        pooled = s / cnt - cur
        mixed = jnp.dot(pooled.astype(BF16), pw_ref[g], preferred_element_type=F32)
        o_ref[:, sl] = (mixed * ps_ref[:, sl]).astype(BF16)


def _pool_prompt(p, pw, ps, batch, tm):
    n = p.shape[0]
    nt = n // batch // tm
    return pl.pallas_call(
        _pool_prompt_kernel, grid=(batch, nt),
        in_specs=[pl.BlockSpec((tm, POOL_WIDTH), lambda b, i: (b * nt + i, 0)),
                  pl.BlockSpec(pw.shape, lambda b, i: (0, 0, 0)),
                  pl.BlockSpec((1, POOL_WIDTH), lambda b, i: (0, 0))],
        out_specs=pl.BlockSpec((tm, POOL_WIDTH), lambda b, i: (b * nt + i, 0)),
        out_shape=jax.ShapeDtypeStruct((n, POOL_WIDTH), BF16),
        scratch_shapes=[pltpu.VMEM((HALO + tm, POOL_WIDTH), F32)],
        compiler_params=_cparams("arbitrary", "arbitrary"), name="pool_prompt",
    )(p, pw, ps)


def _pool_sample_kernel(st_ref, p_ref, pw_ref, ps_ref, o_ref):
    for g, w in enumerate(POOL_WINDOWS):
        sl = slice(g * POOL_GROUP_DIM, (g + 1) * POOL_GROUP_DIM)
        cur = p_ref[:, sl]
        s = cur
        for j in range(1, w):
            s = s + st_ref[POOL_STATE_LEN - j, :, sl]
        pooled = s / float(w) - cur
        mixed = jnp.dot(pooled.astype(BF16), pw_ref[g], preferred_element_type=F32)
        o_ref[:, sl] = (mixed * ps_ref[:, sl]).astype(BF16)


def _pool_sample(state_t, p, pw, ps):
    n = p.shape[0]
    return pl.pallas_call(
        _pool_sample_kernel,
        out_shape=jax.ShapeDtypeStruct((n, POOL_WIDTH), BF16),
        compiler_params=pltpu.CompilerParams(vmem_limit_bytes=VMEM_LIMIT), name="pool_sample",
    )(state_t, p, pw, ps)


def _sort_key(x):
    b = lax.bitcast_convert_type(x, I32)
    return b ^ ((b >> 31) & np.int32(0x7FFFFFFF))


def _count_ge(skey_ref, n_chunks, chunk, cand):
    rows = skey_ref.shape[0]
    candb = jnp.broadcast_to(cand, (rows, LANES))

    def body(c, acc):
        off = pl.multiple_of(c * chunk, chunk)
        blk = skey_ref[:, pl.ds(off, chunk)]
        for j in range(chunk // LANES):
            acc = acc + jnp.where(blk[:, j * LANES:(j + 1) * LANES] >= candb, 1.0, 0.0)
        return acc

    acc = lax.fori_loop(0, n_chunks, body, jnp.zeros((rows, LANES), F32))
    return jnp.sum(acc, axis=1, keepdims=True)


def _select_threshold(skey_ref, n_chunks, chunk, k_top):
    rows = skey_ref.shape[0]
    kf = float(k_top)

    def bit_step(i, prefix):
        cand_u = prefix | lax.shift_left(jnp.int32(1), jnp.asarray(31 - i, I32))
        cnt = _count_ge(skey_ref, n_chunks, chunk, cand_u ^ INT_MIN)
        return jnp.where(cnt >= kf, cand_u, prefix)

    prefix = lax.fori_loop(0, 32, bit_step, jnp.zeros((rows, 1), I32))
    tau = prefix ^ INT_MIN
    n_ge = _count_ge(skey_ref, n_chunks, chunk, tau)
    tie_row = (n_ge > kf) & (tau > NEG_INF_KEY)
    any_tie = jnp.max(jnp.where(tie_row, 1.0, 0.0)) > 0.0

    @pl.when(any_tie)
    def _():
        n_gt = _count_ge(skey_ref, n_chunks, chunk, tau + 1)
        need = kf - n_gt
        r = lax.broadcasted_iota(I32, (chunk, chunk), 0)
        c = lax.broadcasted_iota(I32, (chunk, chunk), 1)
        before = jnp.where(r < c, 1.0, 0.0).astype(BF16)

        def body(ci, seen):
            off = pl.multiple_of(ci * chunk, chunk)
            blk = skey_ref[:, pl.ds(off, chunk)]
            eq = blk == tau
            eqf = jnp.where(eq, 1.0, 0.0)
            rank = seen + jnp.dot(eqf.astype(BF16), before, preferred_element_type=F32)
            drop = eq & (rank >= need) & tie_row
            skey_ref[:, pl.ds(off, chunk)] = jnp.where(drop, tau - 1, blk)
            return seen + jnp.sum(eqf, axis=1, keepdims=True)

        lax.fori_loop(0, n_chunks, body, jnp.zeros((rows, 1), F32))

    return jnp.maximum(tau, NEG_INF_KEY + 1)


DSA_TQ = 128
DSA_TK = 512


def _dsa_prompt_kernel(k_top, qb0, q_ref, qi_ref, wi_ref, k_ref, v_ref, ki_ref, o_ref,
                       skey_ref, qm_ref, m_ref, l_ref, acc_ref):
    tq, tk = DSA_TQ, DSA_TK
    qb = qb0 + pl.program_id(1)
    n_chunks = (qb * tq) // tk + 1
    t_row = qb * tq + lax.broadcasted_iota(I32, (tq, 1), 0)

    qi = qi_ref[...].reshape(N_IDX_HEADS * tq, IDX_HEAD_DIM)
    wi = wi_ref[...]

    def score_chunk(c, carry):
        off = pl.multiple_of(c * tk, tk)
        dots = lax.dot_general(qi, ki_ref[pl.ds(off, tk), :], (((1,), (1,)), ((), ())),
                               preferred_element_type=F32)
        sc = None
        for h in range(N_IDX_HEADS):
            term = jnp.maximum(dots[h * tq:(h + 1) * tq], 0.0) * wi[:, h:h + 1]
            sc = term if sc is None else sc + term
        key_pos = off + lax.broadcasted_iota(I32, (1, tk), 1)
        sc = jnp.where(key_pos <= t_row, sc, -jnp.inf)
        skey_ref[:, pl.ds(off, tk)] = _sort_key(sc)
        return carry

    lax.fori_loop(0, n_chunks, score_chunk, 0)

    thr = _select_threshold(skey_ref, n_chunks, tk, k_top)

    lane = lax.broadcasted_iota(I32, (1, LANES), 1)
    low = lane < HEAD_DIM
    n_pairs = N_HEADS // 2
    for p in range(n_pairs):
        slab = q_ref[:, p * LANES:(p + 1) * LANES]
        qm_ref[p, 0:tq, :] = jnp.where(low, slab, jnp.zeros_like(slab))
        qm_ref[p, tq:2 * tq, :] = jnp.where(low, jnp.zeros_like(slab), slab)
    m_ref[...] = jnp.full(m_ref.shape, MASK_VALUE, F32)
    l_ref[...] = jnp.zeros(l_ref.shape, F32)
    acc_ref[...] = jnp.zeros(acc_ref.shape, F32)

    def attn_chunk(c, carry):
        off = pl.multiple_of(c * tk, tk)
        bias = jnp.where(skey_ref[:, pl.ds(off, tk)] >= thr, 0.0, MASK_VALUE)
        for p in range(n_pairs):
            sl = slice(p * LANES, (p + 1) * LANES)
            s2 = lax.dot_general(qm_ref[p], k_ref[pl.ds(off, tk), sl], (((1,), (1,)), ((), ())),
                                 preferred_element_type=F32)
            probs = []
            for hh in range(2):
                rs = slice(hh * tq, (hh + 1) * tq)
                s = s2[rs] + bias
                m_old = m_ref[p, rs, :]
                m_new = jnp.maximum(m_old, jnp.max(s, axis=1, keepdims=True))
                alpha = jnp.exp(m_old - m_new)
                lsum = alpha * l_ref[p, rs, :]
                parts = []
                for j in range(tk // LANES):
                    e = jnp.exp(s[:, j * LANES:(j + 1) * LANES] - m_new)
                    lsum = lsum + e
                    parts.append(e.astype(BF16))
                m_ref[p, rs, :] = m_new
                l_ref[p, rs, :] = lsum
                acc_ref[p, rs, :] = alpha * acc_ref[p, rs, :]
                probs.append(jnp.concatenate(parts, axis=1))
            pv = jnp.dot(jnp.concatenate(probs, axis=0), v_ref[pl.ds(off, tk), sl],
                         preferred_element_type=F32)
            acc_ref[p] = acc_ref[p] + pv
        return carry

    lax.fori_loop(0, n_chunks, attn_chunk, 0)

    for p in range(n_pairs):
        outs = []
        for hh in range(2):
            rs = slice(hh * tq, (hh + 1) * tq)
            denom = jnp.sum(l_ref[p, rs, :], axis=1, keepdims=True)
            outs.append(acc_ref[p, rs, :] / denom)
        o_ref[:, p * LANES:(p + 1) * LANES] = jnp.where(low, outs[0], outs[1]).astype(BF16)


def _dsa_prompt(q, qi, wi, kb, vb, kib, b, row0, rows, seq):
    nq = seq // DSA_TQ
    qb0 = row0 // DSA_TQ
    k_top = min(TOPK_MAX, seq // 4)
    blk = lambda _, i: (b * nq + qb0 + i, 0)
    per_batch = lambda _, i: (b, 0)
    n_pairs = N_HEADS // 2
    return pl.pallas_call(
        functools.partial(_dsa_prompt_kernel, k_top, qb0), grid=(1, rows // DSA_TQ),
        in_specs=[pl.BlockSpec((DSA_TQ, ATT_WIDTH), blk),
                  pl.BlockSpec((N_IDX_HEADS, DSA_TQ, IDX_HEAD_DIM), lambda _, i: (0, b * nq + qb0 + i, 0)),
                  pl.BlockSpec((DSA_TQ, N_IDX_HEADS), blk),
                  pl.BlockSpec((seq, ATT_WIDTH), per_batch),
                  pl.BlockSpec((seq, ATT_WIDTH), per_batch),
                  pl.BlockSpec((seq, IDX_HEAD_DIM), per_batch)],
        out_specs=pl.BlockSpec((DSA_TQ, ATT_WIDTH), lambda _, i: (i, 0)),
        out_shape=jax.ShapeDtypeStruct((rows, ATT_WIDTH), BF16),
        scratch_shapes=[pltpu.VMEM((DSA_TQ, seq), I32),
                        pltpu.VMEM((n_pairs, 2 * DSA_TQ, LANES), BF16),
                        pltpu.VMEM((n_pairs, 2 * DSA_TQ, LANES), F32),
                        pltpu.VMEM((n_pairs, 2 * DSA_TQ, LANES), F32),
                        pltpu.VMEM((n_pairs, 2 * DSA_TQ, LANES), F32)],
        compiler_params=_cparams("arbitrary", "arbitrary"), name="dsa_prompt",
    )(q, qi, wi, kb, vb, kib)


IDX_PAGES = 16
ATT_PAGES = 8


def _idx_sample_kernel(pt_ref, qi_ref, wi_ref, kin_ref, *refs):
    pages = refs[:IDX_PAGES]
    sc_ref, scn_ref = refs[IDX_PAGES], refs[IDX_PAGES + 1]
    qi = qi_ref[0]
    wi = wi_ref[0]
    nt = (((1,), (1,)), ((), ()))

    def score(keys):
        dots = lax.dot_general(qi, keys, nt, preferred_element_type=F32)
        return jnp.sum(jnp.maximum(dots, 0.0) * wi, axis=0, keepdims=True)

    for j in range(IDX_PAGES):
        sc_ref[0, :, j * PAGE_SIZE:(j + 1) * PAGE_SIZE] = score(pages[j][0].astype(BF16))

    @pl.when(pl.program_id(1) == 0)
    def _():
        new = score(jnp.broadcast_to(kin_ref[0], (8, IDX_HEAD_DIM)))
        lane = lax.broadcasted_iota(I32, (1, LANES), 1)
        scn_ref[0] = jnp.where(lane == 0, jnp.broadcast_to(new[:, 0:1], (1, LANES)), -jnp.inf)


def _idx_sample(page_table, qi_s, wi_s, ki_new, cache_idx_k):
    db, n_pages = page_table.shape
    nj = n_pages // IDX_PAGES
    past = n_pages * PAGE_SIZE

    def page_spec(j):
        return pl.BlockSpec((1, PAGE_SIZE, IDX_HEAD_DIM),
                            lambda b, s, pt, j=j: (pt[b, s * IDX_PAGES + j], 0, 0))

    grid_spec = pltpu.PrefetchScalarGridSpec(
        num_scalar_prefetch=1, grid=(db, nj),
        in_specs=[pl.BlockSpec((1, N_IDX_HEADS, IDX_HEAD_DIM), lambda b, s, pt: (b, 0, 0)),
                  pl.BlockSpec((1, N_IDX_HEADS, 1), lambda b, s, pt: (b, 0, 0)),
                  pl.BlockSpec((1, 1, IDX_HEAD_DIM), lambda b, s, pt: (b, 0, 0))]
                 + [page_spec(j) for j in range(IDX_PAGES)],
        out_specs=[pl.BlockSpec((1, 1, IDX_PAGES * PAGE_SIZE), lambda b, s, pt: (b, 0, s)),
                   pl.BlockSpec((1, 1, LANES), lambda b, s, pt: (b, 0, 0))])
    return pl.pallas_call(
        _idx_sample_kernel, grid_spec=grid_spec,
        out_shape=(jax.ShapeDtypeStruct((db, 1, past), F32),
                   jax.ShapeDtypeStruct((db, 1, LANES), F32)),
        compiler_params=_cparams("arbitrary", "arbitrary"), name="idx_sample",
    )(page_table, qi_s, wi_s, ki_new, *([cache_idx_k] * IDX_PAGES))


def _bias_sample_kernel(k_top, sc_ref, scn_ref, bias_ref, skey_ref, thr_ref):
    past = sc_ref.shape[1]
    width = past + LANES
    ci = pl.program_id(0)

    @pl.when(ci == 0)
    def _():
        skey_ref[:, 0:past] = _sort_key(sc_ref[...])
        skey_ref[:, past:width] = _sort_key(scn_ref[...])
        thr = _select_threshold(skey_ref, width // LANES, LANES, k_top)
        thr_ref[...] = jnp.broadcast_to(thr, thr_ref.shape)

    r = lax.broadcasted_iota(I32, (LANES, LANES * N_HEADS), 0)
    c = lax.broadcasted_iota(I32, (LANES, LANES * N_HEADS), 1)
    expand = jnp.where(c // N_HEADS == r, 1.0, 0.0).astype(BF16)
    off = pl.multiple_of(ci * LANES, LANES)
    bias = jnp.where(skey_ref[:, pl.ds(off, LANES)] >= thr_ref[...], 0.0, MASK_VALUE).astype(BF16)
    bias_ref[0] = jnp.dot(bias, expand, preferred_element_type=F32)


def _bias_sample(scores, score_new, k_top):
    db, past = scores.shape
    width = past + LANES
    nblk = width // LANES
    return pl.pallas_call(
        functools.partial(_bias_sample_kernel, k_top), grid=(nblk,),
        in_specs=[pl.BlockSpec((db, past), lambda i: (0, 0)), pl.BlockSpec((db, LANES), lambda i: (0, 0))],
        out_specs=pl.BlockSpec((1, db, LANES * N_HEADS), lambda i: (i, 0, 0)),
        out_shape=jax.ShapeDtypeStruct((nblk, db, LANES * N_HEADS), F32),
        scratch_shapes=[pltpu.VMEM((db, width), I32), pltpu.VMEM((db, LANES), I32)],
        compiler_params=_cparams("arbitrary"), name="bias_sample",
    )(scores, score_new)


def _head_mask(n_heads, head_dim):
    width = n_heads * head_dim
    h = lax.broadcasted_iota(I32, (8, width), 0)
    l = lax.broadcasted_iota(I32, (8, width), 1)
    return (l // head_dim) == h


PAGE_ROWS = PAGE_SIZE * N_HEADS


def _lanes_to_column(x):
    sub = lax.broadcasted_iota(I32, x.shape, 0)
    lane = lax.broadcasted_iota(I32, x.shape, 1)
    return jnp.sum(jnp.where(sub == lane, x, 0.0), axis=1, keepdims=True)


def _attn_sample_kernel(pt_ref, q_ref, kn_ref, vn_ref, *refs):
    nb = ATT_PAGES
    bias = refs[:nb]
    biasn_ref = refs[nb]
    kp = refs[nb + 1:2 * nb + 1]
    vp = refs[2 * nb + 1:3 * nb + 1]
    o_ref = refs[3 * nb + 1]
    m_ref, l_ref, acc_ref = refs[3 * nb + 2:]
    s_id = pl.program_id(1)
    nt = (((1,), (1,)), ((), ()))
    q = q_ref[0]
    sub = lax.broadcasted_iota(I32, (N_HEADS, PAGE_ROWS), 0)
    col = lax.broadcasted_iota(I32, (N_HEADS, PAGE_ROWS), 1)
    own = (col % N_HEADS) == sub
    n_tiles = PAGE_ROWS // LANES

    def fold(x, op):
        out = x[:, 0:LANES]
        for t in range(1, n_tiles):
            out = op(out, x[:, t * LANES:(t + 1) * LANES])
        return out

    def across_keys(x, op):
        for sh in (8, 16, 32, 64):
            x = op(x, pltpu.roll(x, sh, 1))
        return x

    @pl.when(s_id == 0)
    def _():
        m_ref[...] = jnp.full(m_ref.shape, MASK_VALUE, F32)
        l_ref[...] = jnp.zeros(l_ref.shape, F32)
        acc_ref[...] = jnp.zeros(acc_ref.shape, F32)

    rows = []
    for j in range(nb):
        kpage = kp[j][0].reshape(PAGE_ROWS, HEAD_DIM).astype(BF16)
        st = lax.dot_general(q, kpage, nt, preferred_element_type=F32)
        rows.append(jnp.sum(jnp.where(own, st, 0.0), axis=0, keepdims=True) + bias[j][0, 0])
    s = jnp.concatenate(rows, axis=0)
    blk_max = fold(jnp.max(s, axis=0, keepdims=True), jnp.maximum)
    blk_max = across_keys(jnp.broadcast_to(blk_max, (N_HEADS, LANES)), jnp.maximum)
    m_old = m_ref[...]
    m_new = jnp.maximum(m_old, blk_max)
    alpha = jnp.exp(m_old - m_new)
    m_ref[...] = m_new
    e = jnp.exp(s - jnp.concatenate([m_new[0:1]] * n_tiles, axis=1))
    l_ref[...] = alpha * l_ref[...] + fold(jnp.sum(e, axis=0, keepdims=True), jnp.add)
    pv = jnp.zeros(acc_ref.shape, F32)
    for j in range(nb):
        pm = jnp.where(own, jnp.broadcast_to(e[j:j + 1], own.shape), 0.0).astype(BF16)
        vpage = vp[j][0].reshape(PAGE_ROWS, HEAD_DIM).astype(BF16)
        pv = pv + jnp.dot(pm, vpage, preferred_element_type=F32)
    acc_ref[...] = _lanes_to_column(alpha) * acc_ref[...] + pv

    @pl.when(s_id == pl.num_programs(1) - 1)
    def _():
        m_col = _lanes_to_column(m_ref[...])
        l_col = _lanes_to_column(across_keys(l_ref[...], jnp.add))
        kn = kn_ref[0].astype(BF16).astype(F32)
        s_new = jnp.sum(q.astype(F32) * kn, axis=1, keepdims=True) + biasn_ref[0, 0, :, 0:1]
        m_fin = jnp.maximum(m_col, s_new)
        a = jnp.exp(m_col - m_fin)
        e_new = jnp.exp(s_new - m_fin)
        vn = vn_ref[0].astype(BF16).astype(F32)
        acc = a * acc_ref[...] + e_new.astype(BF16).astype(F32) * vn
        o_ref[0] = (acc / (a * l_col + e_new)).astype(BF16)


def _attn_sample(page_table, q_s, k_new, v_new, bias, cache_k, cache_v):
    db, n_pages = page_table.shape
    nj = n_pages // ATT_PAGES

    def page_spec(j):
        return pl.BlockSpec((1, PAGE_SIZE, N_HEADS, HEAD_DIM),
                            lambda b, s, pt, j=j: (pt[b, s * ATT_PAGES + j], 0, 0, 0))

    def bias_spec(j):
        return pl.BlockSpec((1, 1, 1, PAGE_ROWS), lambda b, s, pt, j=j: (s * ATT_PAGES + j, b, 0, 0))

    tok = lambda b, s, pt: (b, 0, 0)
    head_blk = pl.BlockSpec((1, N_HEADS, HEAD_DIM), tok)
    grid_spec = pltpu.PrefetchScalarGridSpec(
        num_scalar_prefetch=1, grid=(db, nj),
        in_specs=[head_blk, head_blk, head_blk]
                 + [bias_spec(j) for j in range(ATT_PAGES)]
                 + [pl.BlockSpec((1, 1, 1, PAGE_ROWS), lambda b, s, pt: (n_pages, b, 0, 0))]
                 + [page_spec(j) for j in range(ATT_PAGES)] * 2,
        out_specs=head_blk,
        scratch_shapes=[pltpu.VMEM((N_HEADS, LANES), F32), pltpu.VMEM((N_HEADS, LANES), F32),
                        pltpu.VMEM((N_HEADS, HEAD_DIM), F32)])
    return pl.pallas_call(
        _attn_sample_kernel, grid_spec=grid_spec,
        out_shape=jax.ShapeDtypeStruct((db, N_HEADS, HEAD_DIM), BF16),
        compiler_params=_cparams("arbitrary", "arbitrary"), name="attn_sample",
    )(page_table, q_s, k_new, v_new, *([bias] * (ATT_PAGES + 1)),
      *([cache_k] * ATT_PAGES), *([cache_v] * ATT_PAGES))


def _outproj_kernel(x_ref, att_ref, pool_ref, wo_ref, g_ref, wq_ref, x1_ref, q_ref):
    x1 = x_ref[...] + jnp.dot(att_ref[...], wo_ref[0:ATT_WIDTH, :], preferred_element_type=F32) \
        + jnp.dot(pool_ref[...], wo_ref[ATT_WIDTH:, :], preferred_element_type=F32)
    x1_ref[...] = x1
    h = _rms(x1, g_ref[...]).astype(BF16)
    q_ref[...] = jnp.dot(h, wq_ref[...], preferred_element_type=F32).astype(BF16)


def _outproj(x, att, pool, wo, g, wq, tm, row0=0):
    n, d = att.shape[0], x.shape[1]
    t0 = row0 // tm
    row = lambda i: (i, 0)
    off = lambda i: (t0 + i, 0)
    full = lambda i: (0, 0)
    return pl.pallas_call(
        _outproj_kernel, grid=(n // tm,),
        in_specs=[pl.BlockSpec((tm, d), off), pl.BlockSpec((tm, ATT_WIDTH), row),
                  pl.BlockSpec((tm, POOL_WIDTH), off), pl.BlockSpec(wo.shape, full),
                  pl.BlockSpec((1, d), full), pl.BlockSpec(wq.shape, full)],
        out_specs=(pl.BlockSpec((tm, d), row), pl.BlockSpec((tm, MEM_WIDTH), row)),
        out_shape=(jax.ShapeDtypeStruct((n, d), F32), jax.ShapeDtypeStruct((n, MEM_WIDTH), BF16)),
        compiler_params=_cparams("parallel"), name="outproj",
    )(x, att, pool, wo, g, wq)


def _memkv_kernel(mem_ref, g_ref, wk_ref, wv_ref, kf_ref, vf_ref, kb_ref, vb_ref):
    m = _rms(mem_ref[...], g_ref[...]).astype(BF16)
    k = jnp.dot(m, wk_ref[...], preferred_element_type=F32)
    v = jnp.dot(m, wv_ref[...], preferred_element_type=F32)
    kf_ref[...] = k
    vf_ref[...] = v
    kb_ref[...] = k.astype(BF16)
    vb_ref[...] = v.astype(BF16)


def _memkv(mem, g, wk, wv, tm):
    n, d = mem.shape
    row = lambda i: (i, 0)
    full = lambda i: (0, 0)
    spec = pl.BlockSpec((tm, MEM_WIDTH), row)
    return pl.pallas_call(
        _memkv_kernel, grid=(n // tm,),
        in_specs=[pl.BlockSpec((tm, d), row), pl.BlockSpec((1, d), full),
                  pl.BlockSpec(wk.shape, full), pl.BlockSpec(wv.shape, full)],
        out_specs=(spec, spec, spec, spec),
        out_shape=(jax.ShapeDtypeStruct((n, MEM_WIDTH), F32), jax.ShapeDtypeStruct((n, MEM_WIDTH), F32),
                   jax.ShapeDtypeStruct((n, MEM_WIDTH), BF16), jax.ShapeDtypeStruct((n, MEM_WIDTH), BF16)),
        compiler_params=_cparams("parallel"), name="memkv",
    )(mem, g, wk, wv)


def _cross_prompt_kernel(q_ref, mk_ref, mv_ref, o_ref):
    nt = (((1,), (1,)), ((), ()))
    for h in range(N_MEM_HEADS):
        sl = slice(h * MEM_HEAD_DIM, (h + 1) * MEM_HEAD_DIM)
        s = lax.dot_general(q_ref[:, sl], mk_ref[:, sl], nt, preferred_element_type=F32)
        s = s * (MEM_HEAD_DIM ** -0.5)
        e = jnp.exp(s - jnp.max(s, axis=1, keepdims=True))
        p = e / jnp.sum(e, axis=1, keepdims=True)
        o_ref[:, sl] = jnp.dot(p.astype(BF16), mv_ref[:, sl], preferred_element_type=F32).astype(BF16)


def _cross_prompt(q, mk, mv, b0, batch, n_mem, tm):
    n = q.shape[0]
    nt = n // batch // tm
    blk = lambda b, i: (b * nt + i, 0)
    per_batch = lambda b, i: (b0 + b, 0)
    return pl.pallas_call(
        _cross_prompt_kernel, grid=(batch, nt),
        in_specs=[pl.BlockSpec((tm, MEM_WIDTH), blk), pl.BlockSpec((n_mem, MEM_WIDTH), per_batch),
                  pl.BlockSpec((n_mem, MEM_WIDTH), per_batch)],
        out_specs=pl.BlockSpec((tm, MEM_WIDTH), blk),
        out_shape=jax.ShapeDtypeStruct((n, MEM_WIDTH), BF16),
        compiler_params=_cparams("parallel", "parallel"), name="cross_prompt",
    )(q, mk, mv)


def _cross_sample_kernel(q_ref, mk_ref, mv_ref, o_ref):
    hm = _head_mask(N_MEM_HEADS, MEM_HEAD_DIM)
    qbd = jnp.where(hm, jnp.broadcast_to(q_ref[0].astype(F32), hm.shape), 0.0).astype(BF16)
    nt = (((1,), (1,)), ((), ()))
    s = lax.dot_general(qbd, mk_ref[0].astype(BF16), nt, preferred_element_type=F32)
    s = s * (MEM_HEAD_DIM ** -0.5)
    e = jnp.exp(s - jnp.max(s, axis=1, keepdims=True))
    p = e / jnp.sum(e, axis=1, keepdims=True)
    o = jnp.dot(p.astype(BF16), mv_ref[0].astype(BF16), preferred_element_type=F32)
    o_ref[0] = jnp.sum(jnp.where(hm, o, 0.0), axis=0, keepdims=True).astype(BF16)


def _cross_sample(q, mem_k, mem_v):
    db, n_mem, _ = mem_k.shape
    tok = lambda b: (b, 0, 0)
    return pl.pallas_call(
        _cross_sample_kernel, grid=(db,),
        in_specs=[pl.BlockSpec((1, 1, MEM_WIDTH), tok), pl.BlockSpec((1, n_mem, MEM_WIDTH), tok),
                  pl.BlockSpec((1, n_mem, MEM_WIDTH), tok)],
        out_specs=pl.BlockSpec((1, 1, MEM_WIDTH), tok),
        out_shape=jax.ShapeDtypeStruct((db, 1, MEM_WIDTH), BF16),
        compiler_params=_cparams("parallel"), name="cross_sample",
    )(q.reshape(db, 1, MEM_WIDTH), mem_k, mem_v)


def _top_rows(s, k):
    rows = s.shape[0]
    row = lax.broadcasted_iota(I32, s.shape, 0)
    vals, idxs = [], []
    for _ in range(k):
        m = jnp.max(s, axis=0, keepdims=True)
        am = jnp.min(jnp.where(s == m, row, rows), axis=0, keepdims=True)
        vals.append(m)
        idxs.append(am)
        s = jnp.where(row == am, -jnp.inf, s)
    return jnp.concatenate(vals, axis=0), jnp.concatenate(idxs, axis=0)


_CAND_ROWS = PEER_TOPK + 8 * 7 + 8


def _route_kernel(x1_ref, o_ref, wmo_ref, g_ref, wpq_ref, sk_ref,
                  x2_ref, h_ref, eid_ref, gate_ref, pq_ref, ts_ref, ti_ref):
    tm = x1_ref.shape[0]
    x2 = x1_ref[...] + jnp.dot(o_ref[...], wmo_ref[...], preferred_element_type=F32)
    x2_ref[...] = x2
    hf = _rms(x2, g_ref[...])
    h_ref[...] = hf
    hb = hf.astype(BF16)
    for hp in range(2 * PEER_HEADS):
        cols = slice(hp * PEER_HALF_DIM, (hp + 1) * PEER_HALF_DIM)
        pq_ref[hp] = jnp.dot(hb, wpq_ref[:, cols], preferred_element_type=F32).astype(BF16)
    nt = (((1,), (1,)), ((), ()))

    def half_topk(hp, carry):
        st = lax.dot_general(sk_ref[hp], pq_ref[hp], nt, preferred_element_type=F32)
        vals, idxs = _top_rows(st, PEER_TOPK)
        ts_ref[hp] = vals
        ti_ref[hp] = idxs
        return carry

    lax.fori_loop(0, 2 * PEER_HEADS, half_topk, 0)

    grp = lax.broadcasted_iota(I32, (8, tm), 0)

    def head_select(h, carry):
        s1, s2 = ts_ref[2 * h], ts_ref[2 * h + 1]
        i1, i2 = ti_ref[2 * h], ti_ref[2 * h + 1]
        cs = [s1[0:1] + s2]
        ce = [i1[0:1] * PEER_N_KEYS + i2]
        for i in range(1, 8):
            valid = grp < (PEER_TOPK // (i + 1))
            cs.append(jnp.where(valid, s1[i:i + 1] + s2[0:8], -jnp.inf))
            ce.append(i1[i:i + 1] * PEER_N_KEYS + i2[0:8])
        cs.append(s1[8:16] + s2[0:1])
        ce.append(i1[8:16] * PEER_N_KEYS + i2[0:1])
        cand_s = jnp.concatenate(cs, axis=0)
        cand_e = jnp.concatenate(ce, axis=0)
        row = lax.broadcasted_iota(I32, cand_s.shape, 0)
        top_s, top_e = [], []
        for _ in range(PEER_TOPK):
            m = jnp.max(cand_s, axis=0, keepdims=True)
            am = jnp.min(jnp.where(cand_s == m, row, _CAND_ROWS), axis=0, keepdims=True)
            hit = row == am
            top_s.append(m)
            top_e.append(jnp.max(jnp.where(hit, cand_e, -1), axis=0, keepdims=True))
            cand_s = jnp.where(hit, -jnp.inf, cand_s)
        top_s = jnp.concatenate(top_s, axis=0)
        e = jnp.exp(top_s - top_s[0:1])
        gate = e / jnp.sum(e, axis=0, keepdims=True)
        r0 = pl.multiple_of(h * PEER_TOPK, PEER_TOPK)
        eid_ref[pl.ds(r0, PEER_TOPK), :] = jnp.concatenate(top_e, axis=0)
        gate_ref[pl.ds(r0, PEER_TOPK), :] = gate
        return carry

    lax.fori_loop(0, PEER_HEADS, head_select, 0)


def _route(x1, o, wmo, g, wpq, sk, tm):
    n, d = x1.shape
    row = lambda i: (i, 0)
    full = lambda i: (0, 0)
    col = lambda i: (0, i)
    return pl.pallas_call(
        _route_kernel, grid=(n // tm,),
        in_specs=[pl.BlockSpec((tm, d), row), pl.BlockSpec((tm, MEM_WIDTH), row),
                  pl.BlockSpec(wmo.shape, full), pl.BlockSpec((1, d), full),
                  pl.BlockSpec(wpq.shape, full), pl.BlockSpec(sk.shape, lambda i: (0, 0, 0))],
        out_specs=(pl.BlockSpec((tm, d), row), pl.BlockSpec((tm, d), row),
                   pl.BlockSpec((N_SEL, tm), col), pl.BlockSpec((N_SEL, tm), col)),
        out_shape=(jax.ShapeDtypeStruct((n, d), F32), jax.ShapeDtypeStruct((n, d), F32),
                   jax.ShapeDtypeStruct((N_SEL, n), I32), jax.ShapeDtypeStruct((N_SEL, n), F32)),
        scratch_shapes=[pltpu.VMEM((2 * PEER_HEADS, tm, PEER_HALF_DIM), BF16),
                        pltpu.VMEM((2 * PEER_HEADS, PEER_TOPK, tm), F32),
                        pltpu.VMEM((2 * PEER_HEADS, PEER_TOPK, tm), I32)],
        compiler_params=_cparams("parallel"), name="route",
    )(x1, o, wmo, g, wpq, sk)


SC_CORES = 2
SC_SUBCORES = 16
SC_LANES = 16
SC_CHUNK = SC_LANES
SC_NBUF = 4
SC_UNROLL = 2
SC_TOKENS = 16


_ERF_P = (-2.72614225801306e-10, 2.77068142495902e-08, -2.10102402082508e-06, -5.69250639462346e-05,
          -7.34990630326855e-04, -2.95459980854025e-03, -1.60960333262415e-02)
_ERF_Q = (-1.45660718464996e-05, -2.13374055278905e-04, -1.68282697438203e-03, -7.37332916720468e-03,
          -1.42647390514189e-02)


def _erf_rational(x):
    x = jnp.minimum(jnp.maximum(x, -4.0), 4.0)
    x2 = x * x
    p = jnp.full_like(x, _ERF_P[0])
    for c in _ERF_P[1:]:
        p = p * x2 + c
    q = jnp.full_like(x, _ERF_Q[0])
    for c in _ERF_Q[1:]:
        q = q * x2 + c
    return x * p / q


def _sc_peer(h, gate, eid, table_u, table_v, tb):
    n, d = h.shape
    n_workers = SC_CORES * SC_SUBCORES
    per_w = n // n_workers
    assert per_w * n_workers == n and per_w % tb == 0
    n_blocks = per_w // tb
    n_chunks = N_SEL // SC_CHUNK
    per_tok = 2 * n_chunks
    items = tb * per_tok
    assert n_chunks % SC_NBUF == 0
    nj = d // SC_LANES
    mesh = plsc.VectorSubcoreMesh(core_axis_name="c", subcore_axis_name="s")

    @functools.partial(
        pl.kernel, mesh=mesh, compiler_params=pltpu.CompilerParams(needs_layout_passes=False),
        out_type=jax.ShapeDtypeStruct((n, d), F32),
        scratch_types=[pltpu.VMEM((tb, d), F32), pltpu.VMEM((tb, N_SEL), F32), pltpu.VMEM((tb, N_SEL), I32),
                       pltpu.VMEM((tb, N_SEL), F32), pltpu.VMEM((tb, d), F32)]
                      + [pltpu.VMEM((SC_CHUNK, d), F32)] * SC_NBUF + [pltpu.SemaphoreType.DMA] * SC_NBUF)
    def sc_kernel(h_hbm, gate_hbm, eid_hbm, u_hbm, v_hbm, out_hbm, h_v, gate_v, idx_v, coef_v, out_v, *rest):
        bufs, sems = rest[:SC_NBUF], rest[SC_NBUF:]
        wid = lax.axis_index("s") * SC_CORES + lax.axis_index("c")
        lane = lax.iota(I32, SC_LANES)

        def gather(tab_hbm, item, b):
            t = item // per_tok
            c = item % n_chunks
            idx = idx_v[t, pl.ds(c * SC_CHUNK, SC_CHUNK)]
            return pltpu.make_async_copy(tab_hbm.at[idx], bufs[b], sems[b])

        def start_item(item, b):
            is_u = (item % per_tok) < n_chunks

            @pl.when(is_u)
            def _():
                gather(u_hbm, item, b).start()

            @pl.when(jnp.logical_not(is_u))
            def _():
                gather(v_hbm, item, b).start()

        def finish_slot(item, b):
            @pl.when(item + SC_NBUF < items)
            def _():
                start_item(item + SC_NBUF, b)

        @pl.loop(0, n_blocks)
        def _(blk):
            tok0 = wid * per_w + blk * tb
            pltpu.sync_copy(h_hbm.at[pl.ds(tok0, tb)], h_v)
            pltpu.sync_copy(gate_hbm.at[pl.ds(tok0, tb)], gate_v)
            pltpu.sync_copy(eid_hbm.at[pl.ds(tok0, tb)], idx_v)
            for b in range(SC_NBUF):
                gather(u_hbm, b, b).start()

            @pl.loop(0, tb)
            def _(t):
                @pl.loop(0, n_chunks, step=SC_NBUF)
                def _(c0):
                    for b in range(SC_NBUF):
                        c = c0 + b
                        item = t * per_tok + c
                        gather(u_hbm, item, b).wait()

                        def dot_step(j, accs):
                            sl = pl.ds(j * SC_LANES, SC_LANES)
                            xj = h_v[t, sl]
                            return tuple(accs[r] + bufs[b][r, sl] * xj for r in range(SC_CHUNK))

                        init = tuple(jnp.zeros((SC_LANES,), F32) for _ in range(SC_CHUNK))
                        accs = plsc.parallel_loop(0, nj, 1, unroll=SC_UNROLL, carry=init)(dot_step)
                        act = jnp.zeros((SC_LANES,), F32)
                        for r in range(SC_CHUNK):
                            act = jnp.where(lane == r, jnp.sum(accs[r]), act)
                        sl = pl.ds(c * SC_CHUNK, SC_CHUNK)
                        gelu = 0.5 * act * (1.0 + _erf_rational(act * (2.0 ** -0.5)))
                        coef_v[t, sl] = gate_v[t, sl] * gelu
                        finish_slot(item, b)

                @pl.loop(0, nj)
                def _(j):
                    out_v[t, pl.ds(j * SC_LANES, SC_LANES)] = jnp.zeros((SC_LANES,), F32)

                @pl.loop(0, n_chunks, step=SC_NBUF)
                def _(c0):
                    for b in range(SC_NBUF):
                        c = c0 + b
                        item = t * per_tok + n_chunks + c
                        gather(v_hbm, item, b).wait()
                        cvec = coef_v[t, pl.ds(c * SC_CHUNK, SC_CHUNK)]
                        splat = [jnp.broadcast_to(jnp.sum(jnp.where(lane == r, cvec, 0.0)), (SC_LANES,))
                                 for r in range(SC_CHUNK)]

                        @plsc.parallel_loop(0, nj, 1, unroll=SC_UNROLL)
                        def _(j):
                            sl = pl.ds(j * SC_LANES, SC_LANES)
                            o = out_v[t, sl]
                            for r in range(SC_CHUNK):
                                o = o + splat[r] * bufs[b][r, sl]
                            out_v[t, sl] = o

                        finish_slot(item, b)

            pltpu.sync_copy(out_v, out_hbm.at[pl.ds(tok0, tb)])

    return sc_kernel(h, gate, eid, table_u, table_v)


def _final_kernel(x2_ref, o_ref, g_ref, y_ref):
    y_ref[...] = _rms(x2_ref[...] + o_ref[...], g_ref[...])


def _final(x2, out, g, tm):
    n, d = x2.shape
    spec = pl.BlockSpec((tm, d), lambda i: (i, 0))
    return pl.pallas_call(
        _final_kernel, grid=(n // tm,),
        in_specs=[spec, spec, pl.BlockSpec((1, d), lambda i: (0, 0))], out_specs=spec,
        out_shape=jax.ShapeDtypeStruct((n, d), F32),
        compiler_params=_cparams("parallel"), name="final_norm",
    )(x2, out, g)


def _peer_ffn_final(x2, h, eid_t, gate_t, peer_u, peer_v, g_final):
    n = x2.shape[0]
    per_w = n // (SC_CORES * SC_SUBCORES)
    tb = SC_TOKENS if per_w % SC_TOKENS == 0 else per_w
    tm = _row_tile(n, 512)
    out = _sc_peer(h, gate_t.T, eid_t.T, peer_u, peer_v, tb)
    return _final(x2, out, g_final, tm)


PROMPT_UNITS_PER_SEQ = 2


def _rope_tables(first_pos, t):
    half = HEAD_DIM // 2
    inv_freq = ROPE_THETA ** (-jnp.arange(half, dtype=F32) / half)
    pos = first_pos + jnp.arange(t, dtype=I32)
    ang = pos.astype(F32)[:, None] * inv_freq[None, :]
    cos, sin = jnp.cos(ang), jnp.sin(ang)
    cos_t = jnp.concatenate([cos, cos, cos, cos], axis=1)
    sin_t = jnp.concatenate([-sin, sin, -sin, sin], axis=1)
    return cos_t, sin_t


def _row_tile(n, pref):
    return pref if n % pref == 0 else n


def kernel(x_prompt, x_sample, mem_prompt, cache_k, cache_v, cache_idx_k, state_pool, cache_mem_k,
           cache_mem_v, page_table, norm_mix, w_in, pool_w, pool_scale, w_out, norm_cross, norm_mem,
           w_mq, w_mk, w_mv, w_mo, norm_ffn, w_pq, sub_keys, peer_u, peer_v, norm_final):
    batch, seq, d = x_prompt.shape
    db, ds, _ = x_sample.shape
    depth = w_in.shape[0]
    assert depth == 1 and ds == 1
    n_pages = page_table.shape[1]
    past = n_pages * PAGE_SIZE
    n_mem = mem_prompt.shape[1]
    n_p, n_s = batch * seq, db * ds
    l = 0

    w = w_in[l]
    c_ki = 4 * ATT_WIDTH
    c_wi = c_ki + IDX_HEAD_DIM
    c_p = c_wi + N_IDX_HEADS
    wbig = jnp.concatenate([w[:, :c_ki], w[:, c_p:]], axis=1).astype(BF16)
    wsm = jnp.pad(w[:, c_ki:c_p], ((0, 0), (0, LANES - (c_p - c_ki)))).astype(BF16)
    g_mix = norm_mix[l][None, :]
    pw = pool_w[l].astype(BF16)
    ps = pool_scale[l][None, :]
    wo = w_out[l].astype(BF16)
    g_cross = norm_cross[l][None, :]
    wq = w_mq[l].astype(BF16)
    wmo = w_mo[l].astype(BF16)
    g_ffn = norm_ffn[l][None, :]
    wpq = w_pq[l].astype(BF16)
    sk = sub_keys[l].reshape(2 * PEER_HEADS, PEER_N_KEYS, PEER_HALF_DIM).astype(BF16)
    g_final = norm_final[None, :]
    pu, pv = peer_u[l], peer_v[l]

    tm = _row_tile(seq, 512)
    cos_p, sin_p = _rope_tables(0, seq)
    xp = x_prompt.reshape(n_p, d)
    q, kf, kb, vf, vb, qi, kif, kib, wi, p = _inproj(xp, g_mix, wbig, wsm, cos_p, sin_p, tm)
    pool_out = _pool_prompt(p, pw, ps, batch, tm)
    mkf, mvf, mkb, mvb = _memkv(mem_prompt.reshape(batch * n_mem, d), norm_mem[l][None, :],
                                w_mk[l].astype(BF16), w_mv[l].astype(BF16), _row_tile(batch * n_mem, 256))
    ys = []
    xs = x_sample.reshape(n_s, d)
    unit = seq // PROMPT_UNITS_PER_SEQ
    for b in range(batch):
        for r0 in range(0, seq, unit):
            att = _dsa_prompt(q, qi, wi, kb, vb, kib, b, r0, unit, seq)
            x1, qm = _outproj(xp, att, pool_out, wo, g_cross, wq, tm, row0=b * seq + r0)
            o = _cross_prompt(qm, mkb, mvb, b, 1, n_mem, tm)
            x2, h, eid_t, gate_t = _route(x1, o, wmo, g_ffn, wpq, sk, _row_tile(unit, 256))
            if not ys:
                eid_t, xs = lax.optimization_barrier((eid_t, xs))
            ys.append(_peer_ffn_final(x2, h, eid_t, gate_t, pu, pv, g_final))
    y_prompt = jnp.concatenate(ys, axis=0).reshape(batch, seq, d)

    new_k_prompt = kf.reshape(1, batch, seq, N_HEADS, HEAD_DIM)
    new_v_prompt = vf.reshape(1, batch, seq, N_HEADS, HEAD_DIM)
    new_idx_k_prompt = kif.reshape(1, batch, seq, IDX_HEAD_DIM)
    new_pool_prompt = p.reshape(batch, seq, POOL_WIDTH)[None, :, seq - POOL_STATE_LEN:, :]
    new_mem_k_prompt = mkf.reshape(1, batch, n_mem, N_MEM_HEADS, MEM_HEAD_DIM)
    new_mem_v_prompt = mvf.reshape(1, batch, n_mem, N_MEM_HEADS, MEM_HEAD_DIM)

    cos_s, sin_s = _rope_tables(past, 1)
    cos_s = jnp.broadcast_to(cos_s, (n_s, LANES))
    sin_s = jnp.broadcast_to(sin_s, (n_s, LANES))
    q, kf, kb, vf, vb, qi, kif, kib, wi, p = _inproj(xs, g_mix, wbig, wsm, cos_s, sin_s, n_s)
    n_pool = cache_k.shape[1]
    qi_s = jnp.transpose(qi, (1, 0, 2))
    scores, score_new = _idx_sample(page_table, qi_s, wi.reshape(n_s, N_IDX_HEADS, 1),
                                    kib.reshape(n_s, 1, IDX_HEAD_DIM), cache_idx_k[l])
    k_top = min(TOPK_MAX, (past + ds) // 4)
    bias = _bias_sample(scores.reshape(n_s, past), score_new.reshape(n_s, LANES), k_top)
    heads = (n_s, N_HEADS, HEAD_DIM)
    att = _attn_sample(page_table, q.reshape(heads), kf.reshape(heads), vf.reshape(heads),
                       bias.reshape(bias.shape[0], n_s, 1, PAGE_ROWS), cache_k[l], cache_v[l]).reshape(n_s, ATT_WIDTH)
    state_t = jnp.transpose(state_pool[l], (1, 0, 2))
    pool_out = _pool_sample(state_t, p, pw, ps)
    x1, qm = _outproj(xs, att, pool_out, wo, g_cross, wq, n_s)
    o = _cross_sample(qm, cache_mem_k[l].reshape(db, n_mem, MEM_WIDTH),
                      cache_mem_v[l].reshape(db, n_mem, MEM_WIDTH)).reshape(n_s, MEM_WIDTH)
    x2, h, eid_t, gate_t = _route(x1, o, wmo, g_ffn, wpq, sk, n_s)
    y_sample = _peer_ffn_final(x2, h, eid_t, gate_t, pu, pv, g_final).reshape(db, ds, d)

    new_k_sample = kf.reshape(1, db, ds, N_HEADS, HEAD_DIM)
    new_v_sample = vf.reshape(1, db, ds, N_HEADS, HEAD_DIM)
    new_idx_k_sample = kif.reshape(1, db, ds, IDX_HEAD_DIM)
    new_pool_sample = jnp.concatenate([state_pool[l][:, 1:, :], p[:, None, :]], axis=1)[None]

    return (y_prompt, y_sample, new_k_prompt, new_v_prompt, new_idx_k_prompt, new_pool_prompt,
            new_mem_k_prompt, new_mem_v_prompt, new_k_sample, new_v_sample, new_idx_k_sample,
            new_pool_sample)
```

```python
import functools

import numpy as np
import jax
import jax.numpy as jnp
from jax import lax
from jax.experimental import pallas as pl
from jax.experimental.pallas import tpu as pltpu
from jax.experimental.pallas import tpu_sc as plsc

F32 = jnp.float32
BF16 = jnp.bfloat16
I32 = jnp.int32

N_HEADS = 8
HEAD_DIM = 64
ATT_WIDTH = N_HEADS * HEAD_DIM
N_IDX_HEADS = 8
IDX_HEAD_DIM = 64
TOPK_MAX = 256
POOL_WINDOWS = (2, 4, 8, 16)
POOL_GROUP_DIM = 128
POOL_WIDTH = POOL_GROUP_DIM * len(POOL_WINDOWS)
POOL_STATE_LEN = max(POOL_WINDOWS) - 1
N_MEM_HEADS = 4
MEM_HEAD_DIM = 128
MEM_WIDTH = N_MEM_HEADS * MEM_HEAD_DIM
PEER_HEADS = 8
PEER_N_KEYS = 128
PEER_HALF_DIM = 128
PEER_TOPK = 16
N_SEL = PEER_HEADS * PEER_TOPK
PAGE_SIZE = 128
ROPE_THETA = 10000.0
RMS_EPS = 1e-6

LANES = 128
MASK_VALUE = -1e30
INT_MIN = np.int32(-2 ** 31)
NEG_INF_KEY = np.int32(np.uint32(0x807FFFFF).astype(np.int64) - 2 ** 32)

VMEM_LIMIT = 56 * 1024 * 1024


def _cparams(*sem):
    return pltpu.CompilerParams(dimension_semantics=sem, vmem_limit_bytes=VMEM_LIMIT)


def _rms(x, g):
    ms = jnp.mean(x * x, axis=-1, keepdims=True)
    return x * lax.rsqrt(ms + RMS_EPS) * g


def _inproj_kernel(x_ref, g_ref, wbig_ref, wsm_ref, cos_ref, sin_ref,
                   q_ref, kf_ref, kb_ref, vf_ref, vb_ref, qi_ref, kif_ref, kib_ref, wi_ref, p_ref):
    h = _rms(x_ref[...], g_ref[...]).astype(BF16)
    cos = cos_ref[...]
    sin = sin_ref[...]
    lane = lax.broadcasted_iota(I32, (1, LANES), 1)
    first_half = (lane % HEAD_DIM) < (HEAD_DIM // 2)

    def rope(z):
        partner = jnp.where(first_half, pltpu.roll(z, LANES - HEAD_DIM // 2, 1),
                            pltpu.roll(z, HEAD_DIM // 2, 1))
        return z * cos + partner * sin

    def proj(c0, width):
        return jnp.dot(h, wbig_ref[:, c0:c0 + width], preferred_element_type=F32)

    for s in range(ATT_WIDTH // LANES):
        sl = slice(s * LANES, (s + 1) * LANES)
        zq = rope(proj(s * LANES, LANES))
        q_ref[:, sl] = (zq * (HEAD_DIM ** -0.5)).astype(BF16)
        zk = rope(proj(ATT_WIDTH + s * LANES, LANES))
        kf_ref[:, sl] = zk
        kb_ref[:, sl] = zk.astype(BF16)
        zqi = rope(proj(3 * ATT_WIDTH + s * LANES, LANES)).astype(BF16)
        qi_ref[2 * s] = zqi[:, :IDX_HEAD_DIM]
        qi_ref[2 * s + 1] = zqi[:, IDX_HEAD_DIM:]
    zv = proj(2 * ATT_WIDTH, ATT_WIDTH)
    vf_ref[...] = zv
    vb_ref[...] = zv.astype(BF16)
    p_ref[...] = proj(4 * ATT_WIDTH, POOL_WIDTH)
    zs = jnp.dot(h, wsm_ref[...], preferred_element_type=F32)
    zki = rope(zs)[:, :IDX_HEAD_DIM]
    kif_ref[...] = zki
    kib_ref[...] = zki.astype(BF16)
    wi_ref[...] = zs[:, IDX_HEAD_DIM:IDX_HEAD_DIM + N_IDX_HEADS] * (
        (N_IDX_HEADS ** -0.5) * (IDX_HEAD_DIM ** -0.5))


def _inproj(x, g, wbig, wsm, cos, sin, tm):
    n, d = x.shape
    nt = n // tm
    nrt = cos.shape[0] // tm
    row = lambda i: (i, 0)
    full = lambda i: (0, 0)
    out_shape = (
        jax.ShapeDtypeStruct((n, ATT_WIDTH), BF16),
        jax.ShapeDtypeStruct((n, ATT_WIDTH), F32),
        jax.ShapeDtypeStruct((n, ATT_WIDTH), BF16),
        jax.ShapeDtypeStruct((n, ATT_WIDTH), F32),
        jax.ShapeDtypeStruct((n, ATT_WIDTH), BF16),
        jax.ShapeDtypeStruct((N_IDX_HEADS, n, IDX_HEAD_DIM), BF16),
        jax.ShapeDtypeStruct((n, IDX_HEAD_DIM), F32),
        jax.ShapeDtypeStruct((n, IDX_HEAD_DIM), BF16),
        jax.ShapeDtypeStruct((n, N_IDX_HEADS), F32),
        jax.ShapeDtypeStruct((n, POOL_WIDTH), F32),
    )
    out_specs = (
        pl.BlockSpec((tm, ATT_WIDTH), row), pl.BlockSpec((tm, ATT_WIDTH), row),
        pl.BlockSpec((tm, ATT_WIDTH), row), pl.BlockSpec((tm, ATT_WIDTH), row),
        pl.BlockSpec((tm, ATT_WIDTH), row),
        pl.BlockSpec((N_IDX_HEADS, tm, IDX_HEAD_DIM), lambda i: (0, i, 0)),
        pl.BlockSpec((tm, IDX_HEAD_DIM), row), pl.BlockSpec((tm, IDX_HEAD_DIM), row),
        pl.BlockSpec((tm, N_IDX_HEADS), row), pl.BlockSpec((tm, POOL_WIDTH), row),
    )
    return pl.pallas_call(
        _inproj_kernel, grid=(nt,),
        in_specs=[pl.BlockSpec((tm, d), row), pl.BlockSpec((1, d), full),
                  pl.BlockSpec(wbig.shape, full), pl.BlockSpec(wsm.shape, full),
                  pl.BlockSpec((tm, LANES), lambda i: (i % nrt, 0)),
                  pl.BlockSpec((tm, LANES), lambda i: (i % nrt, 0))],
        out_specs=out_specs, out_shape=out_shape,
        compiler_params=_cparams("parallel"), name="inproj",
    )(x, g, wbig, wsm, cos, sin)


HALO = 16


def _pool_prompt_kernel(p_ref, pw_ref, ps_ref, o_ref, ext_ref):
    tm = p_ref.shape[0]
    i = pl.program_id(1)

    @pl.when(i == 0)
    def _():
        ext_ref[0:HALO, :] = jnp.zeros((HALO, POOL_WIDTH), F32)

    @pl.when(i > 0)
    def _():
        ext_ref[0:HALO, :] = ext_ref[tm:tm + HALO, :]

    ext_ref[HALO:HALO + tm, :] = p_ref[...]
    pos = i * tm + lax.broadcasted_iota(I32, (tm, 1), 0)
    for g, w in enumerate(POOL_WINDOWS):
        sl = slice(g * POOL_GROUP_DIM, (g + 1) * POOL_GROUP_DIM)
        cur = ext_ref[HALO:HALO + tm, sl]
        s = cur
        for j in range(1, w):
            s = s + ext_ref[HALO - j:HALO - j + tm, sl]
        cnt = jnp.minimum(w, pos + 1).astype(F32)
        pooled = s / cnt - cur
        mixed = jnp.dot(pooled.astype(BF16), pw_ref[g], preferred_element_type=F32)
        o_ref[:, sl] = (mixed * ps_ref[:, sl]).astype(BF16)


def _pool_prompt(p, pw, ps, batch, tm):
    n = p.shape[0]
    nt = n // batch // tm
    return pl.pallas_call(
        _pool_prompt_kernel, grid=(batch, nt),
        in_specs=[pl.BlockSpec((tm, POOL_WIDTH), lambda b, i: (b * nt + i, 0)),
                  pl.BlockSpec(pw.shape, lambda b, i: (0, 0, 0)),
                  pl.BlockSpec((1, POOL_WIDTH), lambda b, i: (0, 0))],
        out_specs=pl.BlockSpec((tm, POOL_WIDTH), lambda b, i: (b * nt + i, 0)),
        out_shape=jax.ShapeDtypeStruct((n, POOL_WIDTH), BF16),
        scratch_shapes=[pltpu.VMEM((HALO + tm, POOL_WIDTH), F32)],
        compiler_params=_cparams("arbitrary", "arbitrary"), name="pool_prompt",
    )(p, pw, ps)


def _pool_sample_kernel(st_ref, p_ref, pw_ref, ps_ref, o_ref):
    for g, w in enumerate(POOL_WINDOWS):
        sl = slice(g * POOL_GROUP_DIM, (g + 1) * POOL_GROUP_DIM)
        cur = p_ref[:, sl]
        s = cur
        for j in range(1, w):
            s = s + st_ref[POOL_STATE_LEN - j, :, sl]
        pooled = s / float(w) - cur
        mixed = jnp.dot(pooled.astype(BF16), pw_ref[g], preferred_element_type=F32)
        o_ref[:, sl] = (mixed * ps_ref[:, sl]).astype(BF16)


def _pool_sample(state_t, p, pw, ps):
    n = p.shape[0]
    return pl.pallas_call(
        _pool_sample_kernel,
        out_shape=jax.ShapeDtypeStruct((n, POOL_WIDTH), BF16),
        compiler_params=pltpu.CompilerParams(vmem_limit_bytes=VMEM_LIMIT), name="pool_sample",
    )(state_t, p, pw, ps)


def _sort_key(x):
    b = lax.bitcast_convert_type(x, I32)
    return b ^ ((b >> 31) & np.int32(0x7FFFFFFF))


def _count_ge(skey_ref, n_chunks, chunk, cand):
    rows = skey_ref.shape[0]
    candb = jnp.broadcast_to(cand, (rows, LANES))

    def body(c, acc):
        off = pl.multiple_of(c * chunk, chunk)
        blk = skey_ref[:, pl.ds(off, chunk)]
        for j in range(chunk // LANES):
            acc = acc + jnp.where(blk[:, j * LANES:(j + 1) * LANES] >= candb, 1.0, 0.0)
        return acc

    acc = lax.fori_loop(0, n_chunks, body, jnp.zeros((rows, LANES), F32))
    return jnp.sum(acc, axis=1, keepdims=True)


def _select_threshold(skey_ref, n_chunks, chunk, k_top):
    rows = skey_ref.shape[0]
    kf = float(k_top)

    def bit_step(i, prefix):
        cand_u = prefix | lax.shift_left(jnp.int32(1), jnp.asarray(31 - i, I32))
        cnt = _count_ge(skey_ref, n_chunks, chunk, cand_u ^ INT_MIN)
        return jnp.where(cnt >= kf, cand_u, prefix)

    prefix = lax.fori_loop(0, 32, bit_step, jnp.zeros((rows, 1), I32))
    tau = prefix ^ INT_MIN
    n_ge = _count_ge(skey_ref, n_chunks, chunk, tau)
    tie_row = (n_ge > kf) & (tau > NEG_INF_KEY)
    any_tie = jnp.max(jnp.where(tie_row, 1.0, 0.0)) > 0.0

    @pl.when(any_tie)
    def _():
        n_gt = _count_ge(skey_ref, n_chunks, chunk, tau + 1)
        need = kf - n_gt
        r = lax.broadcasted_iota(I32, (chunk, chunk), 0)
        c = lax.broadcasted_iota(I32, (chunk, chunk), 1)
        before = jnp.where(r < c, 1.0, 0.0).astype(BF16)

        def body(ci, seen):
            off = pl.multiple_of(ci * chunk, chunk)
            blk = skey_ref[:, pl.ds(off, chunk)]
            eq = blk == tau
            eqf = jnp.where(eq, 1.0, 0.0)
            rank = seen + jnp.dot(eqf.astype(BF16), before, preferred_element_type=F32)
            drop = eq & (rank >= need) & tie_row
            skey_ref[:, pl.ds(off, chunk)] = jnp.where(drop, tau - 1, blk)
            return seen + jnp.sum(eqf, axis=1, keepdims=True)

        lax.fori_loop(0, n_chunks, body, jnp.zeros((rows, 1), F32))

    return jnp.maximum(tau, NEG_INF_KEY + 1)


DSA_TQ = 128
DSA_TK = 512


def _dsa_prompt_kernel(k_top, qb0, q_ref, qi_ref, wi_ref, k_ref, v_ref, ki_ref, o_ref,
                       skey_ref, qm_ref, m_ref, l_ref, acc_ref):
    tq, tk = DSA_TQ, DSA_TK
    qb = qb0 + pl.program_id(1)
    n_chunks = (qb * tq) // tk + 1
    t_row = qb * tq + lax.broadcasted_iota(I32, (tq, 1), 0)

    qi = qi_ref[...].reshape(N_IDX_HEADS * tq, IDX_HEAD_DIM)
    wi = wi_ref[...]

    def score_chunk(c, carry):
        off = pl.multiple_of(c * tk, tk)
        dots = lax.dot_general(qi, ki_ref[pl.ds(off, tk), :], (((1,), (1,)), ((), ())),
                               preferred_element_type=F32)
        sc = None
        for h in range(N_IDX_HEADS):
            term = jnp.maximum(dots[h * tq:(h + 1) * tq], 0.0) * wi[:, h:h + 1]
            sc = term if sc is None else sc + term
        key_pos = off + lax.broadcasted_iota(I32, (1, tk), 1)
        sc = jnp.where(key_pos <= t_row, sc, -jnp.inf)
        skey_ref[:, pl.ds(off, tk)] = _sort_key(sc)
        return carry

    lax.fori_loop(0, n_chunks, score_chunk, 0)

    thr = _select_threshold(skey_ref, n_chunks, tk, k_top)

    lane = lax.broadcasted_iota(I32, (1, LANES), 1)
    low = lane < HEAD_DIM
    n_pairs = N_HEADS // 2
    for p in range(n_pairs):
        slab = q_ref[:, p * LANES:(p + 1) * LANES]
        qm_ref[p, 0:tq, :] = jnp.where(low, slab, jnp.zeros_like(slab))
        qm_ref[p, tq:2 * tq, :] = jnp.where(low, jnp.zeros_like(slab), slab)
    m_ref[...] = jnp.full(m_ref.shape, MASK_VALUE, F32)
    l_ref[...] = jnp.zeros(l_ref.shape, F32)
    acc_ref[...] = jnp.zeros(acc_ref.shape, F32)

    def attn_chunk(c, carry):
        off = pl.multiple_of(c * tk, tk)
        bias = jnp.where(skey_ref[:, pl.ds(off, tk)] >= thr, 0.0, MASK_VALUE)
        for p in range(n_pairs):
            sl = slice(p * LANES, (p + 1) * LANES)
            s2 = lax.dot_general(qm_ref[p], k_ref[pl.ds(off, tk), sl], (((1,), (1,)), ((), ())),
                                 preferred_element_type=F32)
            probs = []
            for hh in range(2):
                rs = slice(hh * tq, (hh + 1) * tq)
                s = s2[rs] + bias
                m_old = m_ref[p, rs, :]
                m_new = jnp.maximum(m_old, jnp.max(s, axis=1, keepdims=True))
                alpha = jnp.exp(m_old - m_new)
                lsum = alpha * l_ref[p, rs, :]
                parts = []
                for j in range(tk // LANES):
                    e = jnp.exp(s[:, j * LANES:(j + 1) * LANES] - m_new)
                    lsum = lsum + e
                    parts.append(e.astype(BF16))
                m_ref[p, rs, :] = m_new
                l_ref[p, rs, :] = lsum
                acc_ref[p, rs, :] = alpha * acc_ref[p, rs, :]
                probs.append(jnp.concatenate(parts, axis=1))
            pv = jnp.dot(jnp.concatenate(probs, axis=0), v_ref[pl.ds(off, tk), sl],
                         preferred_element_type=F32)
            acc_ref[p] = acc_ref[p] + pv
        return carry

    lax.fori_loop(0, n_chunks, attn_chunk, 0)

    for p in range(n_pairs):
        outs = []
        for hh in range(2):
            rs = slice(hh * tq, (hh + 1) * tq)
            denom = jnp.sum(l_ref[p, rs, :], axis=1, keepdims=True)
            outs.append(acc_ref[p, rs, :] / denom)
        o_ref[:, p * LANES:(p + 1) * LANES] = jnp.where(low, outs[0], outs[1]).astype(BF16)


def _dsa_prompt(q, qi, wi, kb, vb, kib, b, row0, rows, seq):
    nq = seq // DSA_TQ
    qb0 = row0 // DSA_TQ
    k_top = min(TOPK_MAX, seq // 4)
    blk = lambda _, i: (b * nq + qb0 + i, 0)
    per_batch = lambda _, i: (b, 0)
    n_pairs = N_HEADS // 2
    return pl.pallas_call(
        functools.partial(_dsa_prompt_kernel, k_top, qb0), grid=(1, rows // DSA_TQ),
        in_specs=[pl.BlockSpec((DSA_TQ, ATT_WIDTH), blk),
                  pl.BlockSpec((N_IDX_HEADS, DSA_TQ, IDX_HEAD_DIM), lambda _, i: (0, b * nq + qb0 + i, 0)),
                  pl.BlockSpec((DSA_TQ, N_IDX_HEADS), blk),
                  pl.BlockSpec((seq, ATT_WIDTH), per_batch),
                  pl.BlockSpec((seq, ATT_WIDTH), per_batch),
                  pl.BlockSpec((seq, IDX_HEAD_DIM), per_batch)],
        out_specs=pl.BlockSpec((DSA_TQ, ATT_WIDTH), lambda _, i: (i, 0)),
        out_shape=jax.ShapeDtypeStruct((rows, ATT_WIDTH), BF16),
        scratch_shapes=[pltpu.VMEM((DSA_TQ, seq), I32),
                        pltpu.VMEM((n_pairs, 2 * DSA_TQ, LANES), BF16),
                        pltpu.VMEM((n_pairs, 2 * DSA_TQ, LANES), F32),
                        pltpu.VMEM((n_pairs, 2 * DSA_TQ, LANES), F32),
                        pltpu.VMEM((n_pairs, 2 * DSA_TQ, LANES), F32)],
        compiler_params=_cparams("arbitrary", "arbitrary"), name="dsa_prompt",
    )(q, qi, wi, kb, vb, kib)


IDX_PAGES = 16
ATT_PAGES = 8


def _idx_sample_kernel(pt_ref, qi_ref, wi_ref, kin_ref, *refs):
    pages = refs[:IDX_PAGES]
    sc_ref, scn_ref = refs[IDX_PAGES], refs[IDX_PAGES + 1]
    qi = qi_ref[0]
    wi = wi_ref[0]
    nt = (((1,), (1,)), ((), ()))

    def weigh(dots):
        return jnp.sum(jnp.maximum(dots, 0.0) * wi, axis=0, keepdims=True)

    for j in range(IDX_PAGES):
        keys_t = pages[j][0].astype(BF16)
        sc_ref[0, :, j * PAGE_SIZE:(j + 1) * PAGE_SIZE] = weigh(
            jnp.dot(qi, keys_t, preferred_element_type=F32))

    @pl.when(pl.program_id(1) == 0)
    def _():
        new = weigh(lax.dot_general(qi, jnp.broadcast_to(kin_ref[0], (8, IDX_HEAD_DIM)), nt,
                                    preferred_element_type=F32))
        lane = lax.broadcasted_iota(I32, (1, LANES), 1)
        scn_ref[0] = jnp.where(lane == 0, jnp.broadcast_to(new[:, 0:1], (1, LANES)), -jnp.inf)


def _idx_sample(page_table, qi_s, wi_s, ki_new, cache_idx_k):
    db, n_pages = page_table.shape
    nj = n_pages // IDX_PAGES
    past = n_pages * PAGE_SIZE

    def page_spec(j):
        return pl.BlockSpec((1, IDX_HEAD_DIM, PAGE_SIZE),
                            lambda b, s, pt, j=j: (pt[b, s * IDX_PAGES + j], 0, 0))

    grid_spec = pltpu.PrefetchScalarGridSpec(
        num_scalar_prefetch=1, grid=(db, nj),
        in_specs=[pl.BlockSpec((1, N_IDX_HEADS, IDX_HEAD_DIM), lambda b, s, pt: (b, 0, 0)),
                  pl.BlockSpec((1, N_IDX_HEADS, 1), lambda b, s, pt: (b, 0, 0)),
                  pl.BlockSpec((1, 1, IDX_HEAD_DIM), lambda b, s, pt: (b, 0, 0))]
                 + [page_spec(j) for j in range(IDX_PAGES)],
        out_specs=[pl.BlockSpec((1, 1, IDX_PAGES * PAGE_SIZE), lambda b, s, pt: (b, 0, s)),
                   pl.BlockSpec((1, 1, LANES), lambda b, s, pt: (b, 0, 0))])
    return pl.pallas_call(
        _idx_sample_kernel, grid_spec=grid_spec,
        out_shape=(jax.ShapeDtypeStruct((db, 1, past), F32),
                   jax.ShapeDtypeStruct((db, 1, LANES), F32)),
        compiler_params=_cparams("arbitrary", "arbitrary"), name="idx_sample",
    )(page_table, qi_s, wi_s, ki_new, *([cache_idx_k] * IDX_PAGES))


def _bias_sample_kernel(k_top, sc_ref, scn_ref, bias_ref, skey_ref):
    past = sc_ref.shape[1]
    width = past + LANES
    skey_ref[:, 0:past] = _sort_key(sc_ref[...])
    skey_ref[:, past:width] = _sort_key(scn_ref[...])
    thr = _select_threshold(skey_ref, width // LANES, LANES, k_top)
    bias_ref[...] = jnp.where(skey_ref[...] >= thr, 0.0, MASK_VALUE)


def _bias_sample(scores, score_new, k_top):
    db, past = scores.shape
    width = past + LANES
    return pl.pallas_call(
        functools.partial(_bias_sample_kernel, k_top),
        out_shape=jax.ShapeDtypeStruct((db, width), F32),
        scratch_shapes=[pltpu.VMEM((db, width), I32)],
        compiler_params=pltpu.CompilerParams(vmem_limit_bytes=VMEM_LIMIT), name="bias_sample",
    )(scores, score_new)


def _head_mask(n_heads, head_dim):
    width = n_heads * head_dim
    h = lax.broadcasted_iota(I32, (8, width), 0)
    l = lax.broadcasted_iota(I32, (8, width), 1)
    return (l // head_dim) == h


def _row_to_column(row):
    n = row.shape[1]
    r = lax.broadcasted_iota(I32, (n, n), 0)
    c = lax.broadcasted_iota(I32, (n, n), 1)
    return jnp.sum(jnp.where(r == c, jnp.broadcast_to(row, (n, n)), 0.0), axis=1, keepdims=True)


def _column_to_row(col):
    n = col.shape[0]
    r = lax.broadcasted_iota(I32, (n, n), 0)
    c = lax.broadcasted_iota(I32, (n, n), 1)
    return jnp.sum(jnp.where(r == c, jnp.broadcast_to(col, (n, n)), 0.0), axis=0, keepdims=True)


def _attn_sample_kernel(pt_ref, q_ref, kn_ref, vn_ref, bias_ref, biasn_ref, *refs):
    nb = ATT_PAGES
    kp = refs[:nb]
    vp = refs[nb:2 * nb]
    o_ref = refs[2 * nb]
    m_ref, l_ref, acc_ref, qcol_ref = refs[2 * nb + 1:]
    s_id = pl.program_id(1)
    tile = (HEAD_DIM, PAGE_SIZE)

    @pl.when(s_id == 0)
    def _():
        m_ref[...] = jnp.full(m_ref.shape, MASK_VALUE, F32)
        l_ref[...] = jnp.zeros(l_ref.shape, F32)
        acc_ref[...] = jnp.zeros(acc_ref.shape, F32)
        q = q_ref[0].astype(F32)
        for h in range(N_HEADS):
            qcol_ref[h] = jnp.broadcast_to(_row_to_column(q[h:h + 1, :]), tile)

    scores = []
    for j in range(nb):
        rows = [jnp.sum(kp[j][0, h] * qcol_ref[h], axis=0, keepdims=True) for h in range(N_HEADS)]
        scores.append(jnp.concatenate(rows, axis=0) + bias_ref[0, :, j * PAGE_SIZE:(j + 1) * PAGE_SIZE])
    blk_max = scores[0]
    for j in range(1, nb):
        blk_max = jnp.maximum(blk_max, scores[j])
    m_old = m_ref[...]
    m_new = jnp.maximum(m_old, jnp.max(blk_max, axis=1, keepdims=True))
    alpha = jnp.exp(m_old - m_new)
    m_ref[...] = m_new
    probs = [jnp.exp(s - m_new) for s in scores]
    lsum = alpha * l_ref[...]
    for e in probs:
        lsum = lsum + e
    l_ref[...] = lsum
    for h in range(N_HEADS):
        acc = jnp.broadcast_to(alpha[h:h + 1, :], tile) * acc_ref[h]
        for j in range(nb):
            acc = acc + jnp.broadcast_to(probs[j][h:h + 1, :], tile) * vp[j][0, h]
        acc_ref[h] = acc

    @pl.when(s_id == pl.num_programs(1) - 1)
    def _():
        m_col = m_ref[:, 0:1]
        l_col = jnp.sum(l_ref[...], axis=1, keepdims=True)
        s_new = jnp.sum(q_ref[0].astype(F32) * kn_ref[0], axis=1, keepdims=True) + biasn_ref[0, :, 0:1]
        m_fin = jnp.maximum(m_col, s_new)
        a = jnp.exp(m_col - m_fin)
        e_new = jnp.exp(s_new - m_fin)
        past_sum = jnp.concatenate(
            [_column_to_row(jnp.sum(acc_ref[h], axis=1, keepdims=True)) for h in range(N_HEADS)], axis=0)
        out = (a * past_sum + e_new * vn_ref[0]) / (a * l_col + e_new)
        o_ref[0] = out.astype(BF16)


def _attn_sample(page_table, q_s, k_new, v_new, bias, cache_k, cache_v):
    db, n_pages = page_table.shape
    nj = n_pages // ATT_PAGES

    def page_spec(j):
        return pl.BlockSpec((1, N_HEADS, HEAD_DIM, PAGE_SIZE),
                            lambda b, s, pt, j=j: (pt[b, s * ATT_PAGES + j], 0, 0, 0))

    tok = lambda b, s, pt: (b, 0, 0)
    head_blk = pl.BlockSpec((1, N_HEADS, HEAD_DIM), tok)
    grid_spec = pltpu.PrefetchScalarGridSpec(
        num_scalar_prefetch=1, grid=(db, nj),
        in_specs=[head_blk, head_blk, head_blk,
                  pl.BlockSpec((1, 1, ATT_PAGES * PAGE_SIZE), lambda b, s, pt: (b, 0, s)),
                  pl.BlockSpec((1, 1, LANES), lambda b, s, pt: (b, 0, n_pages))]
                 + [page_spec(j) for j in range(ATT_PAGES)] * 2,
        out_specs=head_blk,
        scratch_shapes=[pltpu.VMEM((N_HEADS, LANES), F32), pltpu.VMEM((N_HEADS, LANES), F32),
                        pltpu.VMEM((N_HEADS, HEAD_DIM, PAGE_SIZE), F32),
                        pltpu.VMEM((N_HEADS, HEAD_DIM, PAGE_SIZE), F32)])
    return pl.pallas_call(
        _attn_sample_kernel, grid_spec=grid_spec,
        out_shape=jax.ShapeDtypeStruct((db, N_HEADS, HEAD_DIM), BF16),
        compiler_params=_cparams("arbitrary", "arbitrary"), name="attn_sample",
    )(page_table, q_s, k_new, v_new, bias, bias, *([cache_k] * ATT_PAGES), *([cache_v] * ATT_PAGES))


def _outproj_kernel(x_ref, att_ref, pool_ref, wo_ref, g_ref, wq_ref, x1_ref, q_ref):
    x1 = x_ref[...] + jnp.dot(att_ref[...], wo_ref[0:ATT_WIDTH, :], preferred_element_type=F32) \
        + jnp.dot(pool_ref[...], wo_ref[ATT_WIDTH:, :], preferred_element_type=F32)
    x1_ref[...] = x1
    h = _rms(x1, g_ref[...]).astype(BF16)
    q_ref[...] = jnp.dot(h, wq_ref[...], preferred_element_type=F32).astype(BF16)


def _outproj(x, att, pool, wo, g, wq, tm, row0=0):
    n, d = att.shape[0], x.shape[1]
    t0 = row0 // tm
    row = lambda i: (i, 0)
    off = lambda i: (t0 + i, 0)
    full = lambda i: (0, 0)
    return pl.pallas_call(
        _outproj_kernel, grid=(n // tm,),
        in_specs=[pl.BlockSpec((tm, d), off), pl.BlockSpec((tm, ATT_WIDTH), row),
                  pl.BlockSpec((tm, POOL_WIDTH), off), pl.BlockSpec(wo.shape, full),
                  pl.BlockSpec((1, d), full), pl.BlockSpec(wq.shape, full)],
        out_specs=(pl.BlockSpec((tm, d), row), pl.BlockSpec((tm, MEM_WIDTH), row)),
        out_shape=(jax.ShapeDtypeStruct((n, d), F32), jax.ShapeDtypeStruct((n, MEM_WIDTH), BF16)),
        compiler_params=_cparams("parallel"), name="outproj",
    )(x, att, pool, wo, g, wq)


def _memkv_kernel(mem_ref, g_ref, wk_ref, wv_ref, kf_ref, vf_ref, kb_ref, vb_ref):
    m = _rms(mem_ref[...], g_ref[...]).astype(BF16)
    k = jnp.dot(m, wk_ref[...], preferred_element_type=F32)
    v = jnp.dot(m, wv_ref[...], preferred_element_type=F32)
    kf_ref[...] = k
    vf_ref[...] = v
    kb_ref[...] = k.astype(BF16)
    vb_ref[...] = v.astype(BF16)


def _memkv(mem, g, wk, wv, tm):
    n, d = mem.shape
    row = lambda i: (i, 0)
    full = lambda i: (0, 0)
    spec = pl.BlockSpec((tm, MEM_WIDTH), row)
    return pl.pallas_call(
        _memkv_kernel, grid=(n // tm,),
        in_specs=[pl.BlockSpec((tm, d), row), pl.BlockSpec((1, d), full),
                  pl.BlockSpec(wk.shape, full), pl.BlockSpec(wv.shape, full)],
        out_specs=(spec, spec, spec, spec),
        out_shape=(jax.ShapeDtypeStruct((n, MEM_WIDTH), F32), jax.ShapeDtypeStruct((n, MEM_WIDTH), F32),
                   jax.ShapeDtypeStruct((n, MEM_WIDTH), BF16), jax.ShapeDtypeStruct((n, MEM_WIDTH), BF16)),
        compiler_params=_cparams("parallel"), name="memkv",
    )(mem, g, wk, wv)


def _cross_prompt_kernel(q_ref, mk_ref, mv_ref, o_ref):
    nt = (((1,), (1,)), ((), ()))
    for h in range(N_MEM_HEADS):
        sl = slice(h * MEM_HEAD_DIM, (h + 1) * MEM_HEAD_DIM)
        s = lax.dot_general(q_ref[:, sl], mk_ref[:, sl], nt, preferred_element_type=F32)
        s = s * (MEM_HEAD_DIM ** -0.5)
        e = jnp.exp(s - jnp.max(s, axis=1, keepdims=True))
        p = e / jnp.sum(e, axis=1, keepdims=True)
        o_ref[:, sl] = jnp.dot(p.astype(BF16), mv_ref[:, sl], preferred_element_type=F32).astype(BF16)


def _cross_prompt(q, mk, mv, b0, batch, n_mem, tm):
    n = q.shape[0]
    nt = n // batch // tm
    blk = lambda b, i: (b * nt + i, 0)
    per_batch = lambda b, i: (b0 + b, 0)
    return pl.pallas_call(
        _cross_prompt_kernel, grid=(batch, nt),
        in_specs=[pl.BlockSpec((tm, MEM_WIDTH), blk), pl.BlockSpec((n_mem, MEM_WIDTH), per_batch),
                  pl.BlockSpec((n_mem, MEM_WIDTH), per_batch)],
        out_specs=pl.BlockSpec((tm, MEM_WIDTH), blk),
        out_shape=jax.ShapeDtypeStruct((n, MEM_WIDTH), BF16),
        compiler_params=_cparams("parallel", "parallel"), name="cross_prompt",
    )(q, mk, mv)


def _cross_sample_kernel(q_ref, mk_ref, mv_ref, o_ref):
    hm = _head_mask(N_MEM_HEADS, MEM_HEAD_DIM)
    qbd = jnp.where(hm, jnp.broadcast_to(q_ref[0].astype(F32), hm.shape), 0.0).astype(BF16)
    nt = (((1,), (1,)), ((), ()))
    s = lax.dot_general(qbd, mk_ref[0].astype(BF16), nt, preferred_element_type=F32)
    s = s * (MEM_HEAD_DIM ** -0.5)
    e = jnp.exp(s - jnp.max(s, axis=1, keepdims=True))
    p = e / jnp.sum(e, axis=1, keepdims=True)
    o = jnp.dot(p.astype(BF16), mv_ref[0].astype(BF16), preferred_element_type=F32)
    o_ref[0] = jnp.sum(jnp.where(hm, o, 0.0), axis=0, keepdims=True).astype(BF16)


def _cross_sample(q, mem_k, mem_v):
    db, n_mem, _ = mem_k.shape
    tok = lambda b: (b, 0, 0)
    return pl.pallas_call(
        _cross_sample_kernel, grid=(db,),
        in_specs=[pl.BlockSpec((1, 1, MEM_WIDTH), tok), pl.BlockSpec((1, n_mem, MEM_WIDTH), tok),
                  pl.BlockSpec((1, n_mem, MEM_WIDTH), tok)],
        out_specs=pl.BlockSpec((1, 1, MEM_WIDTH), tok),
        out_shape=jax.ShapeDtypeStruct((db, 1, MEM_WIDTH), BF16),
        compiler_params=_cparams("parallel"), name="cross_sample",
    )(q.reshape(db, 1, MEM_WIDTH), mem_k, mem_v)


def _top_rows(s, k):
    rows = s.shape[0]
    row = lax.broadcasted_iota(I32, s.shape, 0)
    vals, idxs = [], []
    for _ in range(k):
        m = jnp.max(s, axis=0, keepdims=True)
        am = jnp.min(jnp.where(s == m, row, rows), axis=0, keepdims=True)
        vals.append(m)
        idxs.append(am)
        s = jnp.where(row == am, -jnp.inf, s)
    return jnp.concatenate(vals, axis=0), jnp.concatenate(idxs, axis=0)


_CAND_ROWS = PEER_TOPK + 8 * 7 + 8


def _route_kernel(x1_ref, o_ref, wmo_ref, g_ref, wpq_ref, sk_ref,
                  x2_ref, h_ref, eid_ref, gate_ref, pq_ref, ts_ref, ti_ref, eid_t_ref, gate_t_ref):
    tm = x1_ref.shape[0]
    x2 = x1_ref[...] + jnp.dot(o_ref[...], wmo_ref[...], preferred_element_type=F32)
    x2_ref[...] = x2
    hf = _rms(x2, g_ref[...])
    h_ref[...] = hf
    hb = hf.astype(BF16)
    for hp in range(2 * PEER_HEADS):
        cols = slice(hp * PEER_HALF_DIM, (hp + 1) * PEER_HALF_DIM)
        pq_ref[hp] = jnp.dot(hb, wpq_ref[:, cols], preferred_element_type=F32).astype(BF16)
    nt = (((1,), (1,)), ((), ()))

    def half_topk(hp, carry):
        st = lax.dot_general(sk_ref[hp], pq_ref[hp], nt, preferred_element_type=F32)
        vals, idxs = _top_rows(st, PEER_TOPK)
        ts_ref[hp] = vals
        ti_ref[hp] = idxs
        return carry

    lax.fori_loop(0, 2 * PEER_HEADS, half_topk, 0)

    grp = lax.broadcasted_iota(I32, (8, tm), 0)

    def head_select(h, carry):
        s1, s2 = ts_ref[2 * h], ts_ref[2 * h + 1]
        i1, i2 = ti_ref[2 * h], ti_ref[2 * h + 1]
        cs = [s1[0:1] + s2]
        ce = [i1[0:1] * PEER_N_KEYS + i2]
        for i in range(1, 8):
            valid = grp < (PEER_TOPK // (i + 1))
            cs.append(jnp.where(valid, s1[i:i + 1] + s2[0:8], -jnp.inf))
            ce.append(i1[i:i + 1] * PEER_N_KEYS + i2[0:8])
        cs.append(s1[8:16] + s2[0:1])
        ce.append(i1[8:16] * PEER_N_KEYS + i2[0:1])
        cand_s = jnp.concatenate(cs, axis=0)
        cand_e = jnp.concatenate(ce, axis=0)
        row = lax.broadcasted_iota(I32, cand_s.shape, 0)
        top_s, top_e = [], []
        for _ in range(PEER_TOPK):
            m = jnp.max(cand_s, axis=0, keepdims=True)
            am = jnp.min(jnp.where(cand_s == m, row, _CAND_ROWS), axis=0, keepdims=True)
            hit = row == am
            top_s.append(m)
            top_e.append(jnp.max(jnp.where(hit, cand_e, -1), axis=0, keepdims=True))
            cand_s = jnp.where(hit, -jnp.inf, cand_s)
        top_s = jnp.concatenate(top_s, axis=0)
        e = jnp.exp(top_s - top_s[0:1])
        gate = e / jnp.sum(e, axis=0, keepdims=True)
        r0 = pl.multiple_of(h * PEER_TOPK, PEER_TOPK)
        eid_t_ref[pl.ds(r0, PEER_TOPK), :] = jnp.concatenate(top_e, axis=0)
        gate_t_ref[pl.ds(r0, PEER_TOPK), :] = gate
        return carry

    lax.fori_loop(0, PEER_HEADS, head_select, 0)
    eid_ref[...] = eid_t_ref[...].T
    gate_ref[...] = gate_t_ref[...].T


def _route(x1, o, wmo, g, wpq, sk, tm):
    n, d = x1.shape
    row = lambda i: (i, 0)
    full = lambda i: (0, 0)
    return pl.pallas_call(
        _route_kernel, grid=(n // tm,),
        in_specs=[pl.BlockSpec((tm, d), row), pl.BlockSpec((tm, MEM_WIDTH), row),
                  pl.BlockSpec(wmo.shape, full), pl.BlockSpec((1, d), full),
                  pl.BlockSpec(wpq.shape, full), pl.BlockSpec(sk.shape, lambda i: (0, 0, 0))],
        out_specs=(pl.BlockSpec((tm, d), row), pl.BlockSpec((tm, d), row),
                   pl.BlockSpec((tm, N_SEL), row), pl.BlockSpec((tm, N_SEL), row)),
        out_shape=(jax.ShapeDtypeStruct((n, d), F32), jax.ShapeDtypeStruct((n, d), F32),
                   jax.ShapeDtypeStruct((n, N_SEL), I32), jax.ShapeDtypeStruct((n, N_SEL), F32)),
        scratch_shapes=[pltpu.VMEM((2 * PEER_HEADS, tm, PEER_HALF_DIM), BF16),
                        pltpu.VMEM((2 * PEER_HEADS, PEER_TOPK, tm), F32),
                        pltpu.VMEM((2 * PEER_HEADS, PEER_TOPK, tm), I32),
                        pltpu.VMEM((N_SEL, tm), I32), pltpu.VMEM((N_SEL, tm), F32)],
        compiler_params=_cparams("parallel"), name="route",
    )(x1, o, wmo, g, wpq, sk)


SC_CORES = 2
SC_SUBCORES = 16
SC_LANES = 16
SC_CHUNK = SC_LANES
SC_NBUF = 4
SC_UNROLL = 2
SC_TOKENS = 16


_ERF_P = (-2.72614225801306e-10, 2.77068142495902e-08, -2.10102402082508e-06, -5.69250639462346e-05,
          -7.34990630326855e-04, -2.95459980854025e-03, -1.60960333262415e-02)
_ERF_Q = (-1.45660718464996e-05, -2.13374055278905e-04, -1.68282697438203e-03, -7.37332916720468e-03,
          -1.42647390514189e-02)


def _erf_rational(x):
    x = jnp.minimum(jnp.maximum(x, -4.0), 4.0)
    x2 = x * x
    p = jnp.full_like(x, _ERF_P[0])
    for c in _ERF_P[1:]:
        p = p * x2 + c
    q = jnp.full_like(x, _ERF_Q[0])
    for c in _ERF_Q[1:]:
        q = q * x2 + c
    return x * p / q


def _sc_peer(h, gate, eid, table_u, table_v, tb):
    n, d = h.shape
    n_workers = SC_CORES * SC_SUBCORES
    per_w = n // n_workers
    assert per_w * n_workers == n and per_w % tb == 0
    n_blocks = per_w // tb
    n_chunks = N_SEL // SC_CHUNK
    per_tok = 2 * n_chunks
    items = tb * per_tok
    assert n_chunks % SC_NBUF == 0
    nj = d // SC_LANES
    mesh = plsc.VectorSubcoreMesh(core_axis_name="c", subcore_axis_name="s")

    @functools.partial(
        pl.kernel, mesh=mesh, compiler_params=pltpu.CompilerParams(needs_layout_passes=False),
        out_type=jax.ShapeDtypeStruct((n, d), F32),
        scratch_types=[pltpu.VMEM((tb, d), F32), pltpu.VMEM((tb, N_SEL), F32), pltpu.VMEM((tb, N_SEL), I32),
                       pltpu.VMEM((tb, N_SEL), F32), pltpu.VMEM((tb, d), F32)]
                      + [pltpu.VMEM((SC_CHUNK, d), F32)] * SC_NBUF + [pltpu.SemaphoreType.DMA] * SC_NBUF)
    def sc_kernel(h_hbm, gate_hbm, eid_hbm, u_hbm, v_hbm, out_hbm, h_v, gate_v, idx_v, coef_v, out_v, *rest):
        bufs, sems = rest[:SC_NBUF], rest[SC_NBUF:]
        wid = lax.axis_index("s") * SC_CORES + lax.axis_index("c")
        lane = lax.iota(I32, SC_LANES)

        def gather(tab_hbm, item, b):
            t = item // per_tok
            c = item % n_chunks
            idx = idx_v[t, pl.ds(c * SC_CHUNK, SC_CHUNK)]
            return pltpu.make_async_copy(tab_hbm.at[idx], bufs[b], sems[b])

        def start_item(item, b):
            is_u = (item % per_tok) < n_chunks

            @pl.when(is_u)
            def _():
                gather(u_hbm, item, b).start()

            @pl.when(jnp.logical_not(is_u))
            def _():
                gather(v_hbm, item, b).start()

        def finish_slot(item, b):
            @pl.when(item + SC_NBUF < items)
            def _():
                start_item(item + SC_NBUF, b)

        @pl.loop(0, n_blocks)
        def _(blk):
            tok0 = wid * per_w + blk * tb
            pltpu.sync_copy(h_hbm.at[pl.ds(tok0, tb)], h_v)
            pltpu.sync_copy(gate_hbm.at[pl.ds(tok0, tb)], gate_v)
            pltpu.sync_copy(eid_hbm.at[pl.ds(tok0, tb)], idx_v)
            for b in range(SC_NBUF):
                gather(u_hbm, b, b).start()

            @pl.loop(0, tb)
            def _(t):
                @pl.loop(0, n_chunks, step=SC_NBUF)
                def _(c0):
                    for b in range(SC_NBUF):
                        c = c0 + b
                        item = t * per_tok + c
                        gather(u_hbm, item, b).wait()

                        def dot_step(j, accs):
                            sl = pl.ds(j * SC_LANES, SC_LANES)
                            xj = h_v[t, sl]
                            return tuple(accs[r] + bufs[b][r, sl] * xj for r in range(SC_CHUNK))

                        init = tuple(jnp.zeros((SC_LANES,), F32) for _ in range(SC_CHUNK))
                        accs = plsc.parallel_loop(0, nj, 1, unroll=SC_UNROLL, carry=init)(dot_step)
                        act = jnp.zeros((SC_LANES,), F32)
                        for r in range(SC_CHUNK):
                            act = jnp.where(lane == r, jnp.sum(accs[r]), act)
                        sl = pl.ds(c * SC_CHUNK, SC_CHUNK)
                        gelu = 0.5 * act * (1.0 + _erf_rational(act * (2.0 ** -0.5)))
                        coef_v[t, sl] = gate_v[t, sl] * gelu
                        finish_slot(item, b)

                @pl.loop(0, nj)
                def _(j):
                    out_v[t, pl.ds(j * SC_LANES, SC_LANES)] = jnp.zeros((SC_LANES,), F32)

                @pl.loop(0, n_chunks, step=SC_NBUF)
                def _(c0):
                    for b in range(SC_NBUF):
                        c = c0 + b
                        item = t * per_tok + n_chunks + c
                        gather(v_hbm, item, b).wait()
                        cvec = coef_v[t, pl.ds(c * SC_CHUNK, SC_CHUNK)]
                        splat = [jnp.broadcast_to(jnp.sum(jnp.where(lane == r, cvec, 0.0)), (SC_LANES,))
                                 for r in range(SC_CHUNK)]

                        @plsc.parallel_loop(0, nj, 1, unroll=SC_UNROLL)
                        def _(j):
                            sl = pl.ds(j * SC_LANES, SC_LANES)
                            o = out_v[t, sl]
                            for r in range(SC_CHUNK):
                                o = o + splat[r] * bufs[b][r, sl]
                            out_v[t, sl] = o

                        finish_slot(item, b)

            pltpu.sync_copy(out_v, out_hbm.at[pl.ds(tok0, tb)])

    return sc_kernel(h, gate, eid, table_u, table_v)


def _final_kernel(x2_ref, o_ref, g_ref, y_ref):
    y_ref[...] = _rms(x2_ref[...] + o_ref[...], g_ref[...])


def _final(x2, out, g, tm):
    n, d = x2.shape
    spec = pl.BlockSpec((tm, d), lambda i: (i, 0))
    return pl.pallas_call(
        _final_kernel, grid=(n // tm,),
        in_specs=[spec, spec, pl.BlockSpec((1, d), lambda i: (0, 0))], out_specs=spec,
        out_shape=jax.ShapeDtypeStruct((n, d), F32),
        compiler_params=_cparams("parallel"), name="final_norm",
    )(x2, out, g)


def _peer_ffn_final(x2, h, eid, gate, peer_u, peer_v, g_final):
    n = x2.shape[0]
    per_w = n // (SC_CORES * SC_SUBCORES)
    tb = SC_TOKENS if per_w % SC_TOKENS == 0 else per_w
    tm = _row_tile(n, 512)
    out = _sc_peer(h, gate, eid, peer_u, peer_v, tb)
    return _final(x2, out, g_final, tm)


PROMPT_UNITS_PER_SEQ = 2


def _rope_tables(first_pos, t):
    half = HEAD_DIM // 2
    inv_freq = ROPE_THETA ** (-jnp.arange(half, dtype=F32) / half)
    pos = first_pos + jnp.arange(t, dtype=I32)
    ang = pos.astype(F32)[:, None] * inv_freq[None, :]
    cos, sin = jnp.cos(ang), jnp.sin(ang)
    cos_t = jnp.concatenate([cos, cos, cos, cos], axis=1)
    sin_t = jnp.concatenate([-sin, sin, -sin, sin], axis=1)
    return cos_t, sin_t


def _row_tile(n, pref):
    return pref if n % pref == 0 else n


def kernel(x_prompt, x_sample, mem_prompt, cache_k, cache_v, cache_idx_k, state_pool, cache_mem_k,
           cache_mem_v, page_table, norm_mix, w_in, pool_w, pool_scale, w_out, norm_cross, norm_mem,
           w_mq, w_mk, w_mv, w_mo, norm_ffn, w_pq, sub_keys, peer_u, peer_v, norm_final):
    batch, seq, d = x_prompt.shape
    db, ds, _ = x_sample.shape
    depth = w_in.shape[0]
    assert depth == 1 and ds == 1
    n_pages = page_table.shape[1]
    past = n_pages * PAGE_SIZE
    n_mem = mem_prompt.shape[1]
    n_p, n_s = batch * seq, db * ds
    l = 0

    w = w_in[l]
    c_ki = 4 * ATT_WIDTH
    c_wi = c_ki + IDX_HEAD_DIM
    c_p = c_wi + N_IDX_HEADS
    wbig = jnp.concatenate([w[:, :c_ki], w[:, c_p:]], axis=1).astype(BF16)
    wsm = jnp.pad(w[:, c_ki:c_p], ((0, 0), (0, LANES - (c_p - c_ki)))).astype(BF16)
    g_mix = norm_mix[l][None, :]
    pw = pool_w[l].astype(BF16)
    ps = pool_scale[l][None, :]
    wo = w_out[l].astype(BF16)
    g_cross = norm_cross[l][None, :]
    wq = w_mq[l].astype(BF16)
    wmo = w_mo[l].astype(BF16)
    g_ffn = norm_ffn[l][None, :]
    wpq = w_pq[l].astype(BF16)
    sk = sub_keys[l].reshape(2 * PEER_HEADS, PEER_N_KEYS, PEER_HALF_DIM).astype(BF16)
    g_final = norm_final[None, :]
    pu, pv = peer_u[l], peer_v[l]

    tm = _row_tile(seq, 512)
    cos_p, sin_p = _rope_tables(0, seq)
    xp = x_prompt.reshape(n_p, d)
    q, kf, kb, vf, vb, qi, kif, kib, wi, p = _inproj(xp, g_mix, wbig, wsm, cos_p, sin_p, tm)
    pool_out = _pool_prompt(p, pw, ps, batch, tm)
    mkf, mvf, mkb, mvb = _memkv(mem_prompt.reshape(batch * n_mem, d), norm_mem[l][None, :],
                                w_mk[l].astype(BF16), w_mv[l].astype(BF16), _row_tile(batch * n_mem, 256))
    ys = []
    xs = x_sample.reshape(n_s, d)
    unit = seq // PROMPT_UNITS_PER_SEQ
    for b in range(batch):
        for r0 in range(0, seq, unit):
            att = _dsa_prompt(q, qi, wi, kb, vb, kib, b, r0, unit, seq)
            x1, qm = _outproj(xp, att, pool_out, wo, g_cross, wq, tm, row0=b * seq + r0)
            o = _cross_prompt(qm, mkb, mvb, b, 1, n_mem, tm)
            x2, h, eid, gate = _route(x1, o, wmo, g_ffn, wpq, sk, _row_tile(unit, 256))
            if not ys:
                eid, xs = lax.optimization_barrier((eid, xs))
            ys.append(_peer_ffn_final(x2, h, eid, gate, pu, pv, g_final))
    y_prompt = jnp.concatenate(ys, axis=0).reshape(batch, seq, d)

    new_k_prompt = kf.reshape(1, batch, seq, N_HEADS, HEAD_DIM)
    new_v_prompt = vf.reshape(1, batch, seq, N_HEADS, HEAD_DIM)
    new_idx_k_prompt = kif.reshape(1, batch, seq, IDX_HEAD_DIM)
    new_pool_prompt = p.reshape(batch, seq, POOL_WIDTH)[None, :, seq - POOL_STATE_LEN:, :]
    new_mem_k_prompt = mkf.reshape(1, batch, n_mem, N_MEM_HEADS, MEM_HEAD_DIM)
    new_mem_v_prompt = mvf.reshape(1, batch, n_mem, N_MEM_HEADS, MEM_HEAD_DIM)

    cos_s, sin_s = _rope_tables(past, 1)
    cos_s = jnp.broadcast_to(cos_s, (n_s, LANES))
    sin_s = jnp.broadcast_to(sin_s, (n_s, LANES))
    q, kf, kb, vf, vb, qi, kif, kib, wi, p = _inproj(xs, g_mix, wbig, wsm, cos_s, sin_s, n_s)
    ck = jnp.transpose(cache_k[l], (0, 2, 3, 1))
    cv = jnp.transpose(cache_v[l], (0, 2, 3, 1))
    cik = jnp.transpose(cache_idx_k[l], (0, 2, 1))
    qi_s = jnp.transpose(qi, (1, 0, 2))
    scores, score_new = _idx_sample(page_table, qi_s, wi.reshape(n_s, N_IDX_HEADS, 1),
                                    kib.reshape(n_s, 1, IDX_HEAD_DIM), cik)
    k_top = min(TOPK_MAX, (past + ds) // 4)
    bias = _bias_sample(scores.reshape(n_s, past), score_new.reshape(n_s, LANES), k_top)
    heads = (n_s, N_HEADS, HEAD_DIM)
    att = _attn_sample(page_table, q.reshape(heads), kf.reshape(heads), vf.reshape(heads),
                       bias.reshape(n_s, 1, past + LANES), ck, cv).reshape(n_s, ATT_WIDTH)
    state_t = jnp.transpose(state_pool[l], (1, 0, 2))
    pool_out = _pool_sample(state_t, p, pw, ps)
    x1, qm = _outproj(xs, att, pool_out, wo, g_cross, wq, n_s)
    o = _cross_sample(qm, cache_mem_k[l].reshape(db, n_mem, MEM_WIDTH),
                      cache_mem_v[l].reshape(db, n_mem, MEM_WIDTH)).reshape(n_s, MEM_WIDTH)
    x2, h, eid, gate = _route(x1, o, wmo, g_ffn, wpq, sk, n_s)
    y_sample = _peer_ffn_final(x2, h, eid, gate, pu, pv, g_final).reshape(db, ds, d)

    new_k_sample = kf.reshape(1, db, ds, N_HEADS, HEAD_DIM)
    new_v_sample = vf.reshape(1, db, ds, N_HEADS, HEAD_DIM)
    new_idx_k_sample = kif.reshape(1, db, ds, IDX_HEAD_DIM)
    new_pool_sample = jnp.concatenate([state_pool[l][:, 1:, :], p[:, None, :]], axis=1)[None]

    return (y_prompt, y_sample, new_k_prompt, new_v_prompt, new_idx_k_prompt, new_pool_prompt,
            new_mem_k_prompt, new_mem_v_prompt, new_k_sample, new_v_sample, new_idx_k_sample,
            new_pool_sample)
```

```python
import functools

import numpy as np
import jax
import jax.numpy as jnp
from jax import lax
from jax.experimental import pallas as pl
from jax.experimental.pallas import tpu as pltpu
from jax.experimental.pallas import tpu_sc as plsc

F32 = jnp.float32
BF16 = jnp.bfloat16
I32 = jnp.int32

N_HEADS = 8
HEAD_DIM = 64
ATT_WIDTH = N_HEADS * HEAD_DIM
N_IDX_HEADS = 8
IDX_HEAD_DIM = 64
TOPK_MAX = 256
POOL_WINDOWS = (2, 4, 8, 16)
POOL_GROUP_DIM = 128
POOL_WIDTH = POOL_GROUP_DIM * len(POOL_WINDOWS)
POOL_STATE_LEN = max(POOL_WINDOWS) - 1
N_MEM_HEADS = 4
MEM_HEAD_DIM = 128
MEM_WIDTH = N_MEM_HEADS * MEM_HEAD_DIM
PEER_HEADS = 8
PEER_N_KEYS = 128
PEER_HALF_DIM = 128
PEER_TOPK = 16
N_SEL = PEER_HEADS * PEER_TOPK
PAGE_SIZE = 128
ROPE_THETA = 10000.0
RMS_EPS = 1e-6

LANES = 128
MASK_VALUE = -1e30
F32_MIN_NORMAL = float(np.finfo(np.float32).tiny)
INT_MIN = np.int32(-2 ** 31)
NEG_INF_KEY = np.int32(np.uint32(0x807FFFFF).astype(np.int64) - 2 ** 32)

VMEM_LIMIT = 56 * 1024 * 1024


def _cparams(*sem):
    return pltpu.CompilerParams(dimension_semantics=sem, vmem_limit_bytes=VMEM_LIMIT)


def _rms(x, g):
    ms = jnp.mean(x * x, axis=-1, keepdims=True)
    return x * lax.rsqrt(ms + RMS_EPS) * g


def _inproj_kernel(x_ref, g_ref, wbig_ref, wsm_ref, cos_ref, sin_ref,
                   q_ref, kf_ref, kb_ref, vf_ref, vb_ref, qi_ref, kif_ref, kib_ref, wi_ref, p_ref):
    h = _rms(x_ref[...], g_ref[...]).astype(BF16)
    cos = cos_ref[...]
    sin = sin_ref[...]
    lane = lax.broadcasted_iota(I32, (1, LANES), 1)
    first_half = (lane % HEAD_DIM) < (HEAD_DIM // 2)

    def rope(z):
        partner = jnp.where(first_half, pltpu.roll(z, LANES - HEAD_DIM // 2, 1),
                            pltpu.roll(z, HEAD_DIM // 2, 1))
        return z * cos + partner * sin

    def proj(c0, width):
        return jnp.dot(h, wbig_ref[:, c0:c0 + width], preferred_element_type=F32)

    for s in range(ATT_WIDTH // LANES):
        sl = slice(s * LANES, (s + 1) * LANES)
        zq = rope(proj(s * LANES, LANES))
        q_ref[:, sl] = (zq * (HEAD_DIM ** -0.5)).astype(BF16)
        zk = rope(proj(ATT_WIDTH + s * LANES, LANES))
        kf_ref[:, sl] = zk
        kb_ref[:, sl] = zk.astype(BF16)
        zqi = rope(proj(3 * ATT_WIDTH + s * LANES, LANES)).astype(BF16)
        qi_ref[2 * s] = zqi[:, :IDX_HEAD_DIM]
        qi_ref[2 * s + 1] = zqi[:, IDX_HEAD_DIM:]
    zv = proj(2 * ATT_WIDTH, ATT_WIDTH)
    vf_ref[...] = zv
    vb_ref[...] = zv.astype(BF16)
    p_ref[...] = proj(4 * ATT_WIDTH, POOL_WIDTH)
    zs = jnp.dot(h, wsm_ref[...], preferred_element_type=F32)
    zki = rope(zs)[:, :IDX_HEAD_DIM]
    kif_ref[...] = zki
    kib_ref[...] = zki.astype(BF16)
    wi_ref[...] = zs[:, IDX_HEAD_DIM:IDX_HEAD_DIM + N_IDX_HEADS] * (
        (N_IDX_HEADS ** -0.5) * (IDX_HEAD_DIM ** -0.5))


def _inproj(x, g, wbig, wsm, cos, sin, tm):
    n, d = x.shape
    nt = n // tm
    nrt = cos.shape[0] // tm
    row = lambda i: (i, 0)
    full = lambda i: (0, 0)
    out_shape = (
        jax.ShapeDtypeStruct((n, ATT_WIDTH), BF16),
        jax.ShapeDtypeStruct((n, ATT_WIDTH), F32),
        jax.ShapeDtypeStruct((n, ATT_WIDTH), BF16),
        jax.ShapeDtypeStruct((n, ATT_WIDTH), F32),
        jax.ShapeDtypeStruct((n, ATT_WIDTH), BF16),
        jax.ShapeDtypeStruct((N_IDX_HEADS, n, IDX_HEAD_DIM), BF16),
        jax.ShapeDtypeStruct((n, IDX_HEAD_DIM), F32),
        jax.ShapeDtypeStruct((n, IDX_HEAD_DIM), BF16),
        jax.ShapeDtypeStruct((n, N_IDX_HEADS), F32),
        jax.ShapeDtypeStruct((n, POOL_WIDTH), F32),
    )
    out_specs = (
        pl.BlockSpec((tm, ATT_WIDTH), row), pl.BlockSpec((tm, ATT_WIDTH), row),
        pl.BlockSpec((tm, ATT_WIDTH), row), pl.BlockSpec((tm, ATT_WIDTH), row),
        pl.BlockSpec((tm, ATT_WIDTH), row),
        pl.BlockSpec((N_IDX_HEADS, tm, IDX_HEAD_DIM), lambda i: (0, i, 0)),
        pl.BlockSpec((tm, IDX_HEAD_DIM), row), pl.BlockSpec((tm, IDX_HEAD_DIM), row),
        pl.BlockSpec((tm, N_IDX_HEADS), row), pl.BlockSpec((tm, POOL_WIDTH), row),
    )
    return pl.pallas_call(
        _inproj_kernel, grid=(nt,),
        in_specs=[pl.BlockSpec((tm, d), row), pl.BlockSpec((1, d), full),
                  pl.BlockSpec(wbig.shape, full), pl.BlockSpec(wsm.shape, full),
                  pl.BlockSpec((tm, LANES), lambda i: (i % nrt, 0)),
                  pl.BlockSpec((tm, LANES), lambda i: (i % nrt, 0))],
        out_specs=out_specs, out_shape=out_shape,
        compiler_params=_cparams("parallel"), name="inproj",
    )(x, g, wbig, wsm, cos, sin)


HALO = 16


def _pool_prompt_kernel(p_ref, pw_ref, ps_ref, o_ref, ext_ref):
    tm = p_ref.shape[0]
    i = pl.program_id(1)

    @pl.when(i == 0)
    def _():
        ext_ref[0:HALO, :] = jnp.zeros((HALO, POOL_WIDTH), F32)

    @pl.when(i > 0)
    def _():
        ext_ref[0:HALO, :] = ext_ref[tm:tm + HALO, :]

    ext_ref[HALO:HALO + tm, :] = p_ref[...]
    pos = i * tm + lax.broadcasted_iota(I32, (tm, 1), 0)
    for g, w in enumerate(POOL_WINDOWS):
        sl = slice(g * POOL_GROUP_DIM, (g + 1) * POOL_GROUP_DIM)
        cur = ext_ref[HALO:HALO + tm, sl]
        s = cur
        for j in range(1, w):
            s = s + ext_ref[HALO - j:HALO - j + tm, sl]
        cnt = jnp.minimum(w, pos + 1).astype(F32)
        pooled = s / cnt - cur
        mixed = jnp.dot(pooled.astype(BF16), pw_ref[g], preferred_element_type=F32)
        o_ref[:, sl] = (mixed * ps_ref[:, sl]).astype(BF16)


def _pool_prompt(p, pw, ps, batch, tm):
    n = p.shape[0]
    nt = n // batch // tm
    return pl.pallas_call(
        _pool_prompt_kernel, grid=(batch, nt),
        in_specs=[pl.BlockSpec((tm, POOL_WIDTH), lambda b, i: (b * nt + i, 0)),
                  pl.BlockSpec(pw.shape, lambda b, i: (0, 0, 0)),
                  pl.BlockSpec((1, POOL_WIDTH), lambda b, i: (0, 0))],
        out_specs=pl.BlockSpec((tm, POOL_WIDTH), lambda b, i: (b * nt + i, 0)),
        out_shape=jax.ShapeDtypeStruct((n, POOL_WIDTH), BF16),
        scratch_shapes=[pltpu.VMEM((HALO + tm, POOL_WIDTH), F32)],
        compiler_params=_cparams("arbitrary", "arbitrary"), name="pool_prompt",
    )(p, pw, ps)


def _pool_sample_kernel(st_ref, p_ref, pw_ref, ps_ref, o_ref):
    for g, w in enumerate(POOL_WINDOWS):
        sl = slice(g * POOL_GROUP_DIM, (g + 1) * POOL_GROUP_DIM)
        cur = p_ref[:, sl]
        s = cur
        for j in range(1, w):
            s = s + st_ref[POOL_STATE_LEN - j, :, sl]
        pooled = s / float(w) - cur
        mixed = jnp.dot(pooled.astype(BF16), pw_ref[g], preferred_element_type=F32)
        o_ref[:, sl] = (mixed * ps_ref[:, sl]).astype(BF16)


def _pool_sample(state_t, p, pw, ps):
    n = p.shape[0]
    return pl.pallas_call(
        _pool_sample_kernel,
        out_shape=jax.ShapeDtypeStruct((n, POOL_WIDTH), BF16),
        compiler_params=pltpu.CompilerParams(vmem_limit_bytes=VMEM_LIMIT), name="pool_sample",
    )(state_t, p, pw, ps)


def _sort_key(x):
    b = lax.bitcast_convert_type(x, I32)
    return b ^ ((b >> 31) & np.int32(0x7FFFFFFF))


def _count_ge(skey_ref, n_chunks, chunk, cand):
    rows = skey_ref.shape[0]
    candb = jnp.broadcast_to(cand, (rows, LANES))

    def body(c, acc):
        off = pl.multiple_of(c * chunk, chunk)
        blk = skey_ref[:, pl.ds(off, chunk)]
        for j in range(chunk // LANES):
            acc = acc + jnp.where(blk[:, j * LANES:(j + 1) * LANES] >= candb, 1.0, 0.0)
        return acc

    acc = lax.fori_loop(0, n_chunks, body, jnp.zeros((rows, LANES), F32))
    return jnp.sum(acc, axis=1, keepdims=True)


def _count_ge_bf16(hi_ref, n_chunks, chunk, cand):
    rows = hi_ref.shape[0]
    candb = jnp.broadcast_to(cand, (rows, LANES))
    one = jnp.ones((rows, LANES), BF16)
    zero = jnp.zeros((rows, LANES), BF16)

    def body(c, acc):
        off = pl.multiple_of(c * chunk, chunk)
        blk = hi_ref[:, pl.ds(off, chunk)]
        for j in range(chunk // LANES):
            acc = acc + jnp.where(blk[:, j * LANES:(j + 1) * LANES] >= candb, one, zero)
        return acc

    acc = lax.fori_loop(0, n_chunks, body, zero)
    return jnp.sum(acc.astype(F32), axis=1, keepdims=True)


def _select_threshold(skey_ref, n_chunks, chunk, k_top, hi_ref=None):
    rows = skey_ref.shape[0]
    kf = float(k_top)

    def bit_step(i, prefix):
        cand_u = prefix | lax.shift_left(jnp.int32(1), jnp.asarray(31 - i, I32))
        cnt = _count_ge(skey_ref, n_chunks, chunk, cand_u ^ INT_MIN)
        return jnp.where(cnt >= kf, cand_u, prefix)

    def hi_step(i, prefix16):
        cand_u = prefix16 | lax.shift_left(jnp.int32(1), jnp.asarray(15 - i, I32))
        h = ((cand_u ^ 0x8000) << 16) >> 16
        bits = (h ^ ((h >> 15) & 0x7FFF)) << 16
        subnormal = (((bits >> 23) & 0xFF) == 0) & ((bits & 0x007F0000) != 0)
        bits = jnp.where(subnormal, jnp.where(bits < 0, 0, 0x00800000), bits)
        cand = lax.bitcast_convert_type(bits, F32).astype(BF16)
        cnt = _count_ge_bf16(hi_ref, n_chunks, chunk, cand)
        return jnp.where(cnt >= kf, cand_u, prefix16)

    prefix = jnp.zeros((rows, 1), I32)
    first_bit = 0
    if hi_ref is not None:
        assert n_chunks is not None and hi_ref.shape[1] // LANES <= 256
        prefix = lax.fori_loop(0, 16, hi_step, prefix) << 16
        first_bit = 16
    prefix = lax.fori_loop(first_bit, 32, bit_step, prefix)
    tau = prefix ^ INT_MIN
    n_ge = _count_ge(skey_ref, n_chunks, chunk, tau)
    tie_row = (n_ge > kf) & (tau > NEG_INF_KEY)
    any_tie = jnp.max(jnp.where(tie_row, 1.0, 0.0)) > 0.0

    @pl.when(any_tie)
    def _():
        n_gt = _count_ge(skey_ref, n_chunks, chunk, tau + 1)
        need = kf - n_gt
        r = lax.broadcasted_iota(I32, (chunk, chunk), 0)
        c = lax.broadcasted_iota(I32, (chunk, chunk), 1)
        before = jnp.where(r < c, 1.0, 0.0).astype(BF16)

        def body(ci, seen):
            off = pl.multiple_of(ci * chunk, chunk)
            blk = skey_ref[:, pl.ds(off, chunk)]
            eq = blk == tau
            eqf = jnp.where(eq, 1.0, 0.0)
            rank = seen + jnp.dot(eqf.astype(BF16), before, preferred_element_type=F32)
            drop = eq & (rank >= need) & tie_row
            skey_ref[:, pl.ds(off, chunk)] = jnp.where(drop, tau - 1, blk)
            return seen + jnp.sum(eqf, axis=1, keepdims=True)

        lax.fori_loop(0, n_chunks, body, jnp.zeros((rows, 1), F32))

    return jnp.maximum(tau, NEG_INF_KEY + 1)


DSA_TQ = 128
DSA_TK = 512


def _dsa_prompt_kernel(k_top, qb0, q_ref, qi_ref, wi_ref, k_ref, v_ref, ki_ref, o_ref,
                       skey_ref, hi_ref, qm_ref, m_ref, l_ref, acc_ref):
    tq, tk = DSA_TQ, DSA_TK
    qb = qb0 + pl.program_id(1)
    n_chunks = (qb * tq) // tk + 1
    t_row = qb * tq + lax.broadcasted_iota(I32, (tq, 1), 0)

    qi = qi_ref[...].reshape(N_IDX_HEADS * tq, IDX_HEAD_DIM)
    wi = wi_ref[...]

    def score_chunk(c, carry):
        off = pl.multiple_of(c * tk, tk)
        dots = lax.dot_general(qi, ki_ref[pl.ds(off, tk), :], (((1,), (1,)), ((), ())),
                               preferred_element_type=F32)
        sc = None
        for h in range(N_IDX_HEADS):
            term = jnp.maximum(dots[h * tq:(h + 1) * tq], 0.0) * wi[:, h:h + 1]
            sc = term if sc is None else sc + term
        key_pos = off + lax.broadcasted_iota(I32, (1, tk), 1)
        sc = jnp.where(key_pos <= t_row, sc, -jnp.inf)
        sc = jnp.where(jnp.abs(sc) < F32_MIN_NORMAL, 0.0, sc)
        skey_ref[:, pl.ds(off, tk)] = _sort_key(sc)
        upper = lax.bitcast_convert_type(sc, I32) & np.int32(-65536)
        hi_ref[:, pl.ds(off, tk)] = lax.bitcast_convert_type(upper, F32).astype(BF16)
        return carry

    lax.fori_loop(0, n_chunks, score_chunk, 0)

    thr = _select_threshold(skey_ref, n_chunks, tk, k_top, hi_ref)

    lane = lax.broadcasted_iota(I32, (1, LANES), 1)
    low = lane < HEAD_DIM
    n_pairs = N_HEADS // 2
    for p in range(n_pairs):
        slab = q_ref[:, p * LANES:(p + 1) * LANES]
        qm_ref[p, 0:tq, :] = jnp.where(low, slab, jnp.zeros_like(slab))
        qm_ref[p, tq:2 * tq, :] = jnp.where(low, jnp.zeros_like(slab), slab)
    m_ref[...] = jnp.full(m_ref.shape, MASK_VALUE, F32)
    l_ref[...] = jnp.zeros(l_ref.shape, F32)
    acc_ref[...] = jnp.zeros(acc_ref.shape, F32)

    def attn_chunk(c, carry):
        off = pl.multiple_of(c * tk, tk)
        bias = jnp.where(skey_ref[:, pl.ds(off, tk)] >= thr, 0.0, MASK_VALUE)
        for p in range(n_pairs):
            sl = slice(p * LANES, (p + 1) * LANES)
            s2 = lax.dot_general(qm_ref[p], k_ref[pl.ds(off, tk), sl], (((1,), (1,)), ((), ())),
                                 preferred_element_type=F32)
            probs = []
            for hh in range(2):
                rs = slice(hh * tq, (hh + 1) * tq)
                s = s2[rs] + bias
                m_old = m_ref[p, rs, :]
                m_new = jnp.maximum(m_old, jnp.max(s, axis=1, keepdims=True))
                alpha = jnp.exp(m_old - m_new)
                lsum = alpha * l_ref[p, rs, :]
                parts = []
                for j in range(tk // LANES):
                    e = jnp.exp(s[:, j * LANES:(j + 1) * LANES] - m_new)
                    lsum = lsum + e
                    parts.append(e.astype(BF16))
                m_ref[p, rs, :] = m_new
                l_ref[p, rs, :] = lsum
                acc_ref[p, rs, :] = alpha * acc_ref[p, rs, :]
                probs.append(jnp.concatenate(parts, axis=1))
            pv = jnp.dot(jnp.concatenate(probs, axis=0), v_ref[pl.ds(off, tk), sl],
                         preferred_element_type=F32)
            acc_ref[p] = acc_ref[p] + pv
        return carry

    lax.fori_loop(0, n_chunks, attn_chunk, 0)

    for p in range(n_pairs):
        outs = []
        for hh in range(2):
            rs = slice(hh * tq, (hh + 1) * tq)
            denom = jnp.sum(l_ref[p, rs, :], axis=1, keepdims=True)
            outs.append(acc_ref[p, rs, :] / denom)
        o_ref[:, p * LANES:(p + 1) * LANES] = jnp.where(low, outs[0], outs[1]).astype(BF16)


def _dsa_prompt(q, qi, wi, kb, vb, kib, b, row0, rows, seq):
    nq = seq // DSA_TQ
    qb0 = row0 // DSA_TQ
    k_top = min(TOPK_MAX, seq // 4)
    blk = lambda _, i: (b * nq + qb0 + i, 0)
    per_batch = lambda _, i: (b, 0)
    n_pairs = N_HEADS // 2
    return pl.pallas_call(
        functools.partial(_dsa_prompt_kernel, k_top, qb0), grid=(1, rows // DSA_TQ),
        in_specs=[pl.BlockSpec((DSA_TQ, ATT_WIDTH), blk),
                  pl.BlockSpec((N_IDX_HEADS, DSA_TQ, IDX_HEAD_DIM), lambda _, i: (0, b * nq + qb0 + i, 0)),
                  pl.BlockSpec((DSA_TQ, N_IDX_HEADS), blk),
                  pl.BlockSpec((seq, ATT_WIDTH), per_batch),
                  pl.BlockSpec((seq, ATT_WIDTH), per_batch),
                  pl.BlockSpec((seq, IDX_HEAD_DIM), per_batch)],
        out_specs=pl.BlockSpec((DSA_TQ, ATT_WIDTH), lambda _, i: (i, 0)),
        out_shape=jax.ShapeDtypeStruct((rows, ATT_WIDTH), BF16),
        scratch_shapes=[pltpu.VMEM((DSA_TQ, seq), I32), pltpu.VMEM((DSA_TQ, seq), BF16),
                        pltpu.VMEM((n_pairs, 2 * DSA_TQ, LANES), BF16),
                        pltpu.VMEM((n_pairs, 2 * DSA_TQ, LANES), F32),
                        pltpu.VMEM((n_pairs, 2 * DSA_TQ, LANES), F32),
                        pltpu.VMEM((n_pairs, 2 * DSA_TQ, LANES), F32)],
        cost_estimate=pl.CostEstimate(
            flops=2 * rows * (row0 + rows // 2) * (N_IDX_HEADS * IDX_HEAD_DIM + 2 * ATT_WIDTH),
            transcendentals=rows * (row0 + rows // 2) * N_HEADS,
            bytes_accessed=2 * (2 * seq * ATT_WIDTH + seq * IDX_HEAD_DIM + 4 * rows * ATT_WIDTH)),
        compiler_params=_cparams("arbitrary", "arbitrary"), name="dsa_prompt",
    )(q, qi, wi, kb, vb, kib)


IDX_PAGES = 16
ATT_PAGES = 8


def _idx_sample_kernel(pt_ref, qi_ref, wi_ref, kin_ref, *refs):
    pages = refs[:IDX_PAGES]
    sc_ref, scn_ref = refs[IDX_PAGES], refs[IDX_PAGES + 1]
    qi = qi_ref[0]
    wi = wi_ref[0]
    nt = (((1,), (1,)), ((), ()))

    def weigh(dots):
        return jnp.sum(jnp.maximum(dots, 0.0) * wi, axis=0, keepdims=True)

    for j in range(IDX_PAGES):
        keys_t = pages[j][0].astype(BF16)
        sc_ref[0, :, j * PAGE_SIZE:(j + 1) * PAGE_SIZE] = weigh(
            jnp.dot(qi, keys_t, preferred_element_type=F32))

    @pl.when(pl.program_id(1) == 0)
    def _():
        new = weigh(lax.dot_general(qi, jnp.broadcast_to(kin_ref[0], (8, IDX_HEAD_DIM)), nt,
                                    preferred_element_type=F32))
        lane = lax.broadcasted_iota(I32, (1, LANES), 1)
        scn_ref[0] = jnp.where(lane == 0, jnp.broadcast_to(new[:, 0:1], (1, LANES)), -jnp.inf)


def _idx_sample(page_table, qi_s, wi_s, ki_new, cache_idx_k):
    db, n_pages = page_table.shape
    nj = n_pages // IDX_PAGES
    past = n_pages * PAGE_SIZE

    def page_spec(j):
        return pl.BlockSpec((1, IDX_HEAD_DIM, PAGE_SIZE),
                            lambda b, s, pt, j=j: (pt[b, s * IDX_PAGES + j], 0, 0))

    grid_spec = pltpu.PrefetchScalarGridSpec(
        num_scalar_prefetch=1, grid=(db, nj),
        in_specs=[pl.BlockSpec((1, N_IDX_HEADS, IDX_HEAD_DIM), lambda b, s, pt: (b, 0, 0)),
                  pl.BlockSpec((1, N_IDX_HEADS, 1), lambda b, s, pt: (b, 0, 0)),
                  pl.BlockSpec((1, 1, IDX_HEAD_DIM), lambda b, s, pt: (b, 0, 0))]
                 + [page_spec(j) for j in range(IDX_PAGES)],
        out_specs=[pl.BlockSpec((1, 1, IDX_PAGES * PAGE_SIZE), lambda b, s, pt: (b, 0, s)),
                   pl.BlockSpec((1, 1, LANES), lambda b, s, pt: (b, 0, 0))])
    return pl.pallas_call(
        _idx_sample_kernel, grid_spec=grid_spec,
        out_shape=(jax.ShapeDtypeStruct((db, 1, past), F32),
                   jax.ShapeDtypeStruct((db, 1, LANES), F32)),
        compiler_params=_cparams("arbitrary", "arbitrary"), name="idx_sample",
    )(page_table, qi_s, wi_s, ki_new, *([cache_idx_k] * IDX_PAGES))


def _bias_sample_kernel(k_top, sc_ref, scn_ref, bias_ref, skey_ref):
    past = sc_ref.shape[1]
    width = past + LANES
    skey_ref[:, 0:past] = _sort_key(sc_ref[...])
    skey_ref[:, past:width] = _sort_key(scn_ref[...])
    thr = _select_threshold(skey_ref, width // LANES, LANES, k_top)
    bias_ref[...] = jnp.where(skey_ref[...] >= thr, 0.0, MASK_VALUE)


def _bias_sample(scores, score_new, k_top):
    db, past = scores.shape
    width = past + LANES
    return pl.pallas_call(
        functools.partial(_bias_sample_kernel, k_top),
        out_shape=jax.ShapeDtypeStruct((db, width), F32),
        scratch_shapes=[pltpu.VMEM((db, width), I32)],
        compiler_params=pltpu.CompilerParams(vmem_limit_bytes=VMEM_LIMIT), name="bias_sample",
    )(scores, score_new)


def _head_mask(n_heads, head_dim):
    width = n_heads * head_dim
    h = lax.broadcasted_iota(I32, (8, width), 0)
    l = lax.broadcasted_iota(I32, (8, width), 1)
    return (l // head_dim) == h


def _row_to_column(row):
    n = row.shape[1]
    r = lax.broadcasted_iota(I32, (n, n), 0)
    c = lax.broadcasted_iota(I32, (n, n), 1)
    return jnp.sum(jnp.where(r == c, jnp.broadcast_to(row, (n, n)), 0.0), axis=1, keepdims=True)


def _column_to_row(col):
    n = col.shape[0]
    r = lax.broadcasted_iota(I32, (n, n), 0)
    c = lax.broadcasted_iota(I32, (n, n), 1)
    return jnp.sum(jnp.where(r == c, jnp.broadcast_to(col, (n, n)), 0.0), axis=0, keepdims=True)


def _attn_sample_kernel(pt_ref, q_ref, kn_ref, vn_ref, bias_ref, biasn_ref, *refs):
    nb = ATT_PAGES
    kp = refs[:nb]
    vp = refs[nb:2 * nb]
    o_ref = refs[2 * nb]
    m_ref, l_ref, acc_ref, qcol_ref = refs[2 * nb + 1:]
    s_id = pl.program_id(1)
    tile = (HEAD_DIM, PAGE_SIZE)

    @pl.when(s_id == 0)
    def _():
        m_ref[...] = jnp.full(m_ref.shape, MASK_VALUE, F32)
        l_ref[...] = jnp.zeros(l_ref.shape, F32)
        acc_ref[...] = jnp.zeros(acc_ref.shape, F32)
        q = q_ref[0].astype(F32)
        for h in range(N_HEADS):
            qcol_ref[h] = jnp.broadcast_to(_row_to_column(q[h:h + 1, :]), tile)

    scores = []
    for j in range(nb):
        rows = [jnp.sum(kp[j][0, h] * qcol_ref[h], axis=0, keepdims=True) for h in range(N_HEADS)]
        scores.append(jnp.concatenate(rows, axis=0) + bias_ref[0, :, j * PAGE_SIZE:(j + 1) * PAGE_SIZE])
    blk_max = scores[0]
    for j in range(1, nb):
        blk_max = jnp.maximum(blk_max, scores[j])
    m_old = m_ref[...]
    m_new = jnp.maximum(m_old, jnp.max(blk_max, axis=1, keepdims=True))
    alpha = jnp.exp(m_old - m_new)
    m_ref[...] = m_new
    probs = [jnp.exp(s - m_new) for s in scores]
    lsum = alpha * l_ref[...]
    for e in probs:
        lsum = lsum + e
    l_ref[...] = lsum
    for h in range(N_HEADS):
        acc = jnp.broadcast_to(alpha[h:h + 1, :], tile) * acc_ref[h]
        for j in range(nb):
            acc = acc + jnp.broadcast_to(probs[j][h:h + 1, :], tile) * vp[j][0, h]
        acc_ref[h] = acc

    @pl.when(s_id == pl.num_programs(1) - 1)
    def _():
        m_col = m_ref[:, 0:1]
        l_col = jnp.sum(l_ref[...], axis=1, keepdims=True)
        s_new = jnp.sum(q_ref[0].astype(F32) * kn_ref[0], axis=1, keepdims=True) + biasn_ref[0, :, 0:1]
        m_fin = jnp.maximum(m_col, s_new)
        a = jnp.exp(m_col - m_fin)
        e_new = jnp.exp(s_new - m_fin)
        past_sum = jnp.concatenate(
            [_column_to_row(jnp.sum(acc_ref[h], axis=1, keepdims=True)) for h in range(N_HEADS)], axis=0)
        out = (a * past_sum + e_new * vn_ref[0]) / (a * l_col + e_new)
        o_ref[0] = out.astype(BF16)


def _attn_sample(page_table, q_s, k_new, v_new, bias, cache_k, cache_v):
    db, n_pages = page_table.shape
    nj = n_pages // ATT_PAGES

    def page_spec(j):
        return pl.BlockSpec((1, N_HEADS, HEAD_DIM, PAGE_SIZE),
                            lambda b, s, pt, j=j: (pt[b, s * ATT_PAGES + j], 0, 0, 0))

    tok = lambda b, s, pt: (b, 0, 0)
    head_blk = pl.BlockSpec((1, N_HEADS, HEAD_DIM), tok)
    grid_spec = pltpu.PrefetchScalarGridSpec(
        num_scalar_prefetch=1, grid=(db, nj),
        in_specs=[head_blk, head_blk, head_blk,
                  pl.BlockSpec((1, 1, ATT_PAGES * PAGE_SIZE), lambda b, s, pt: (b, 0, s)),
                  pl.BlockSpec((1, 1, LANES), lambda b, s, pt: (b, 0, n_pages))]
                 + [page_spec(j) for j in range(ATT_PAGES)] * 2,
        out_specs=head_blk,
        scratch_shapes=[pltpu.VMEM((N_HEADS, LANES), F32), pltpu.VMEM((N_HEADS, LANES), F32),
                        pltpu.VMEM((N_HEADS, HEAD_DIM, PAGE_SIZE), F32),
                        pltpu.VMEM((N_HEADS, HEAD_DIM, PAGE_SIZE), F32)])
    return pl.pallas_call(
        _attn_sample_kernel, grid_spec=grid_spec,
        out_shape=jax.ShapeDtypeStruct((db, N_HEADS, HEAD_DIM), BF16),
        compiler_params=_cparams("arbitrary", "arbitrary"), name="attn_sample",
    )(page_table, q_s, k_new, v_new, bias, bias, *([cache_k] * ATT_PAGES), *([cache_v] * ATT_PAGES))


def _outproj_kernel(x_ref, att_ref, pool_ref, wo_ref, g_ref, wq_ref, x1_ref, q_ref):
    x1 = x_ref[...] + jnp.dot(att_ref[...], wo_ref[0:ATT_WIDTH, :], preferred_element_type=F32) \
        + jnp.dot(pool_ref[...], wo_ref[ATT_WIDTH:, :], preferred_element_type=F32)
    x1_ref[...] = x1
    h = _rms(x1, g_ref[...]).astype(BF16)
    q_ref[...] = jnp.dot(h, wq_ref[...], preferred_element_type=F32).astype(BF16)


def _outproj(x, att, pool, wo, g, wq, tm, row0=0):
    n, d = att.shape[0], x.shape[1]
    t0 = row0 // tm
    row = lambda i: (i, 0)
    off = lambda i: (t0 + i, 0)
    full = lambda i: (0, 0)
    return pl.pallas_call(
        _outproj_kernel, grid=(n // tm,),
        in_specs=[pl.BlockSpec((tm, d), off), pl.BlockSpec((tm, ATT_WIDTH), row),
                  pl.BlockSpec((tm, POOL_WIDTH), off), pl.BlockSpec(wo.shape, full),
                  pl.BlockSpec((1, d), full), pl.BlockSpec(wq.shape, full)],
        out_specs=(pl.BlockSpec((tm, d), row), pl.BlockSpec((tm, MEM_WIDTH), row)),
        out_shape=(jax.ShapeDtypeStruct((n, d), F32), jax.ShapeDtypeStruct((n, MEM_WIDTH), BF16)),
        compiler_params=_cparams("parallel"), name="outproj",
    )(x, att, pool, wo, g, wq)


def _memkv_kernel(mem_ref, g_ref, wk_ref, wv_ref, kf_ref, vf_ref, kb_ref, vb_ref):
    m = _rms(mem_ref[...], g_ref[...]).astype(BF16)
    k = jnp.dot(m, wk_ref[...], preferred_element_type=F32)
    v = jnp.dot(m, wv_ref[...], preferred_element_type=F32)
    kf_ref[...] = k
    vf_ref[...] = v
    kb_ref[...] = k.astype(BF16)
    vb_ref[...] = v.astype(BF16)


def _memkv(mem, g, wk, wv, tm):
    n, d = mem.shape
    row = lambda i: (i, 0)
    full = lambda i: (0, 0)
    spec = pl.BlockSpec((tm, MEM_WIDTH), row)
    return pl.pallas_call(
        _memkv_kernel, grid=(n // tm,),
        in_specs=[pl.BlockSpec((tm, d), row), pl.BlockSpec((1, d), full),
                  pl.BlockSpec(wk.shape, full), pl.BlockSpec(wv.shape, full)],
        out_specs=(spec, spec, spec, spec),
        out_shape=(jax.ShapeDtypeStruct((n, MEM_WIDTH), F32), jax.ShapeDtypeStruct((n, MEM_WIDTH), F32),
                   jax.ShapeDtypeStruct((n, MEM_WIDTH), BF16), jax.ShapeDtypeStruct((n, MEM_WIDTH), BF16)),
        compiler_params=_cparams("parallel"), name="memkv",
    )(mem, g, wk, wv)


def _cross_prompt_kernel(q_ref, mk_ref, mv_ref, o_ref):
    nt = (((1,), (1,)), ((), ()))
    for h in range(N_MEM_HEADS):
        sl = slice(h * MEM_HEAD_DIM, (h + 1) * MEM_HEAD_DIM)
        s = lax.dot_general(q_ref[:, sl], mk_ref[:, sl], nt, preferred_element_type=F32)
        s = s * (MEM_HEAD_DIM ** -0.5)
        e = jnp.exp(s - jnp.max(s, axis=1, keepdims=True))
        p = e / jnp.sum(e, axis=1, keepdims=True)
        o_ref[:, sl] = jnp.dot(p.astype(BF16), mv_ref[:, sl], preferred_element_type=F32).astype(BF16)


def _cross_prompt(q, mk, mv, b0, batch, n_mem, tm):
    n = q.shape[0]
    nt = n // batch // tm
    blk = lambda b, i: (b * nt + i, 0)
    per_batch = lambda b, i: (b0 + b, 0)
    return pl.pallas_call(
        _cross_prompt_kernel, grid=(batch, nt),
        in_specs=[pl.BlockSpec((tm, MEM_WIDTH), blk), pl.BlockSpec((n_mem, MEM_WIDTH), per_batch),
                  pl.BlockSpec((n_mem, MEM_WIDTH), per_batch)],
        out_specs=pl.BlockSpec((tm, MEM_WIDTH), blk),
        out_shape=jax.ShapeDtypeStruct((n, MEM_WIDTH), BF16),
        compiler_params=_cparams("parallel", "parallel"), name="cross_prompt",
    )(q, mk, mv)


def _cross_sample_kernel(q_ref, mk_ref, mv_ref, o_ref):
    hm = _head_mask(N_MEM_HEADS, MEM_HEAD_DIM)
    qbd = jnp.where(hm, jnp.broadcast_to(q_ref[0].astype(F32), hm.shape), 0.0).astype(BF16)
    nt = (((1,), (1,)), ((), ()))
    s = lax.dot_general(qbd, mk_ref[0].astype(BF16), nt, preferred_element_type=F32)
    s = s * (MEM_HEAD_DIM ** -0.5)
    e = jnp.exp(s - jnp.max(s, axis=1, keepdims=True))
    p = e / jnp.sum(e, axis=1, keepdims=True)
    o = jnp.dot(p.astype(BF16), mv_ref[0].astype(BF16), preferred_element_type=F32)
    o_ref[0] = jnp.sum(jnp.where(hm, o, 0.0), axis=0, keepdims=True).astype(BF16)


def _cross_sample(q, mem_k, mem_v):
    db, n_mem, _ = mem_k.shape
    tok = lambda b: (b, 0, 0)
    return pl.pallas_call(
        _cross_sample_kernel, grid=(db,),
        in_specs=[pl.BlockSpec((1, 1, MEM_WIDTH), tok), pl.BlockSpec((1, n_mem, MEM_WIDTH), tok),
                  pl.BlockSpec((1, n_mem, MEM_WIDTH), tok)],
        out_specs=pl.BlockSpec((1, 1, MEM_WIDTH), tok),
        out_shape=jax.ShapeDtypeStruct((db, 1, MEM_WIDTH), BF16),
        compiler_params=_cparams("parallel"), name="cross_sample",
    )(q.reshape(db, 1, MEM_WIDTH), mem_k, mem_v)


def _top_rows(s, k):
    rows = s.shape[0]
    row = lax.broadcasted_iota(I32, s.shape, 0)
    vals, idxs = [], []
    for _ in range(k):
        m = jnp.max(s, axis=0, keepdims=True)
        am = jnp.min(jnp.where(s == m, row, rows), axis=0, keepdims=True)
        vals.append(m)
        idxs.append(am)
        s = jnp.where(row == am, -jnp.inf, s)
    return jnp.concatenate(vals, axis=0), jnp.concatenate(idxs, axis=0)


_CAND_ROWS = PEER_TOPK + 8 * 7 + 8


def _route_kernel(x1_ref, o_ref, wmo_ref, g_ref, wpq_ref, sk_ref,
                  x2_ref, h_ref, eid_ref, gate_ref, pq_ref, ts_ref, ti_ref, eid_t_ref, gate_t_ref):
    tm = x1_ref.shape[0]
    x2 = x1_ref[...] + jnp.dot(o_ref[...], wmo_ref[...], preferred_element_type=F32)
    x2_ref[...] = x2
    hf = _rms(x2, g_ref[...])
    h_ref[...] = hf
    hb = hf.astype(BF16)
    for hp in range(2 * PEER_HEADS):
        cols = slice(hp * PEER_HALF_DIM, (hp + 1) * PEER_HALF_DIM)
        pq_ref[hp] = jnp.dot(hb, wpq_ref[:, cols], preferred_element_type=F32).astype(BF16)
    nt = (((1,), (1,)), ((), ()))

    def half_topk(hp, carry):
        st = lax.dot_general(sk_ref[hp], pq_ref[hp], nt, preferred_element_type=F32)
        vals, idxs = _top_rows(st, PEER_TOPK)
        ts_ref[hp] = vals
        ti_ref[hp] = idxs
        return carry

    lax.fori_loop(0, 2 * PEER_HEADS, half_topk, 0)

    grp = lax.broadcasted_iota(I32, (8, tm), 0)

    def head_select(h, carry):
        s1, s2 = ts_ref[2 * h], ts_ref[2 * h + 1]
        i1, i2 = ti_ref[2 * h], ti_ref[2 * h + 1]
        cs = [s1[0:1] + s2]
        ce = [i1[0:1] * PEER_N_KEYS + i2]
        for i in range(1, 8):
            valid = grp < (PEER_TOPK // (i + 1))
            cs.append(jnp.where(valid, s1[i:i + 1] + s2[0:8], -jnp.inf))
            ce.append(i1[i:i + 1] * PEER_N_KEYS + i2[0:8])
        cs.append(s1[8:16] + s2[0:1])
        ce.append(i1[8:16] * PEER_N_KEYS + i2[0:1])
        cand_s = jnp.concatenate(cs, axis=0)
        cand_e = jnp.concatenate(ce, axis=0)
        row = lax.broadcasted_iota(I32, cand_s.shape, 0)
        top_s, top_e = [], []
        for _ in range(PEER_TOPK):
            m = jnp.max(cand_s, axis=0, keepdims=True)
            am = jnp.min(jnp.where(cand_s == m, row, _CAND_ROWS), axis=0, keepdims=True)
            hit = row == am
            top_s.append(m)
            top_e.append(jnp.max(jnp.where(hit, cand_e, -1), axis=0, keepdims=True))
            cand_s = jnp.where(hit, -jnp.inf, cand_s)
        top_s = jnp.concatenate(top_s, axis=0)
        e = jnp.exp(top_s - top_s[0:1])
        gate = e / jnp.sum(e, axis=0, keepdims=True)
        r0 = pl.multiple_of(h * PEER_TOPK, PEER_TOPK)
        eid_t_ref[pl.ds(r0, PEER_TOPK), :] = jnp.concatenate(top_e, axis=0)
        gate_t_ref[pl.ds(r0, PEER_TOPK), :] = gate
        return carry

    lax.fori_loop(0, PEER_HEADS, head_select, 0)
    eid_ref[...] = eid_t_ref[...].T
    gate_ref[...] = gate_t_ref[...].T


def _route(x1, o, wmo, g, wpq, sk, tm):
    n, d = x1.shape
    row = lambda i: (i, 0)
    full = lambda i: (0, 0)
    return pl.pallas_call(
        _route_kernel, grid=(n // tm,),
        in_specs=[pl.BlockSpec((tm, d), row), pl.BlockSpec((tm, MEM_WIDTH), row),
                  pl.BlockSpec(wmo.shape, full), pl.BlockSpec((1, d), full),
                  pl.BlockSpec(wpq.shape, full), pl.BlockSpec(sk.shape, lambda i: (0, 0, 0))],
        out_specs=(pl.BlockSpec((tm, d), row), pl.BlockSpec((tm, d), row),
                   pl.BlockSpec((tm, N_SEL), row), pl.BlockSpec((tm, N_SEL), row)),
        out_shape=(jax.ShapeDtypeStruct((n, d), F32), jax.ShapeDtypeStruct((n, d), F32),
                   jax.ShapeDtypeStruct((n, N_SEL), I32), jax.ShapeDtypeStruct((n, N_SEL), F32)),
        scratch_shapes=[pltpu.VMEM((2 * PEER_HEADS, tm, PEER_HALF_DIM), BF16),
                        pltpu.VMEM((2 * PEER_HEADS, PEER_TOPK, tm), F32),
                        pltpu.VMEM((2 * PEER_HEADS, PEER_TOPK, tm), I32),
                        pltpu.VMEM((N_SEL, tm), I32), pltpu.VMEM((N_SEL, tm), F32)],
        cost_estimate=pl.CostEstimate(
            flops=2 * n * (MEM_WIDTH * d + d * wpq.shape[1] + wpq.shape[1] * PEER_N_KEYS),
            transcendentals=n * N_SEL,
            bytes_accessed=4 * n * (3 * d + 2 * N_SEL) + 2 * n * MEM_WIDTH + 2 * (wmo.size + wpq.size + sk.size)),
        compiler_params=_cparams("parallel"), name="route",
    )(x1, o, wmo, g, wpq, sk)


SC_CORES = 2
SC_SUBCORES = 16
SC_LANES = 16
SC_CHUNK = SC_LANES
SC_NBUF = 4
SC_UNROLL = 2
SC_TOKENS = 16


_ERF_P = (-2.72614225801306e-10, 2.77068142495902e-08, -2.10102402082508e-06, -5.69250639462346e-05,
          -7.34990630326855e-04, -2.95459980854025e-03, -1.60960333262415e-02)
_ERF_Q = (-1.45660718464996e-05, -2.13374055278905e-04, -1.68282697438203e-03, -7.37332916720468e-03,
          -1.42647390514189e-02)


def _erf_rational(x):
    x = jnp.minimum(jnp.maximum(x, -4.0), 4.0)
    x2 = x * x
    p = jnp.full_like(x, _ERF_P[0])
    for c in _ERF_P[1:]:
        p = p * x2 + c
    q = jnp.full_like(x, _ERF_Q[0])
    for c in _ERF_Q[1:]:
        q = q * x2 + c
    return x * p / q


def _sc_peer(h, gate, eid, table_u, table_v, tb):
    n, d = h.shape
    n_workers = SC_CORES * SC_SUBCORES
    per_w = n // n_workers
    assert per_w * n_workers == n and per_w % tb == 0
    n_blocks = per_w // tb
    n_chunks = N_SEL // SC_CHUNK
    per_tok = 2 * n_chunks
    items = tb * per_tok
    assert n_chunks % SC_NBUF == 0
    nj = d // SC_LANES
    mesh = plsc.VectorSubcoreMesh(core_axis_name="c", subcore_axis_name="s")

    @functools.partial(
        pl.kernel, mesh=mesh, compiler_params=pltpu.CompilerParams(needs_layout_passes=False),
        cost_estimate=pl.CostEstimate(flops=4 * n * N_SEL * d, transcendentals=0,
                                      bytes_accessed=4 * (2 * n * N_SEL * d + 2 * n * d + 2 * n * N_SEL)),
        out_type=jax.ShapeDtypeStruct((n, d), F32),
        scratch_types=[pltpu.VMEM((tb, d), F32), pltpu.VMEM((tb, N_SEL), F32), pltpu.VMEM((tb, N_SEL), I32),
                       pltpu.VMEM((tb, N_SEL), F32), pltpu.VMEM((tb, d), F32)]
                      + [pltpu.VMEM((SC_CHUNK, d), F32)] * SC_NBUF + [pltpu.SemaphoreType.DMA] * SC_NBUF)
    def sc_kernel(h_hbm, gate_hbm, eid_hbm, u_hbm, v_hbm, out_hbm, h_v, gate_v, idx_v, coef_v, out_v, *rest):
        bufs, sems = rest[:SC_NBUF], rest[SC_NBUF:]
        wid = lax.axis_index("s") * SC_CORES + lax.axis_index("c")
        lane = lax.iota(I32, SC_LANES)

        def gather(tab_hbm, item, b):
            t = item // per_tok
            c = item % n_chunks
            idx = idx_v[t, pl.ds(c * SC_CHUNK, SC_CHUNK)]
            return pltpu.make_async_copy(tab_hbm.at[idx], bufs[b], sems[b])

        def start_item(item, b):
            is_u = (item % per_tok) < n_chunks

            @pl.when(is_u)
            def _():
                gather(u_hbm, item, b).start()

            @pl.when(jnp.logical_not(is_u))
            def _():
                gather(v_hbm, item, b).start()

        def finish_slot(item, b):
            @pl.when(item + SC_NBUF < items)
            def _():
                start_item(item + SC_NBUF, b)

        @pl.loop(0, n_blocks)
        def _(blk):
            tok0 = wid * per_w + blk * tb
            pltpu.sync_copy(h_hbm.at[pl.ds(tok0, tb)], h_v)
            pltpu.sync_copy(gate_hbm.at[pl.ds(tok0, tb)], gate_v)
            pltpu.sync_copy(eid_hbm.at[pl.ds(tok0, tb)], idx_v)
            for b in range(SC_NBUF):
                gather(u_hbm, b, b).start()

            @pl.loop(0, tb)
            def _(t):
                @pl.loop(0, n_chunks, step=SC_NBUF)
                def _(c0):
                    for b in range(SC_NBUF):
                        c = c0 + b
                        item = t * per_tok + c
                        gather(u_hbm, item, b).wait()

                        def dot_step(j, accs):
                            sl = pl.ds(j * SC_LANES, SC_LANES)
                            xj = h_v[t, sl]
                            return tuple(accs[r] + bufs[b][r, sl] * xj for r in range(SC_CHUNK))

                        init = tuple(jnp.zeros((SC_LANES,), F32) for _ in range(SC_CHUNK))
                        accs = plsc.parallel_loop(0, nj, 1, unroll=SC_UNROLL, carry=init)(dot_step)
                        act = jnp.zeros((SC_LANES,), F32)
                        for r in range(SC_CHUNK):
                            act = jnp.where(lane == r, jnp.sum(accs[r]), act)
                        sl = pl.ds(c * SC_CHUNK, SC_CHUNK)
                        gelu = 0.5 * act * (1.0 + _erf_rational(act * (2.0 ** -0.5)))
                        coef_v[t, sl] = gate_v[t, sl] * gelu
                        finish_slot(item, b)

                @pl.loop(0, nj)
                def _(j):
                    out_v[t, pl.ds(j * SC_LANES, SC_LANES)] = jnp.zeros((SC_LANES,), F32)

                @pl.loop(0, n_chunks, step=SC_NBUF)
                def _(c0):
                    for b in range(SC_NBUF):
                        c = c0 + b
                        item = t * per_tok + n_chunks + c
                        gather(v_hbm, item, b).wait()
                        cvec = coef_v[t, pl.ds(c * SC_CHUNK, SC_CHUNK)]
                        splat = [jnp.broadcast_to(jnp.sum(jnp.where(lane == r, cvec, 0.0)), (SC_LANES,))
                                 for r in range(SC_CHUNK)]

                        @plsc.parallel_loop(0, nj, 1, unroll=SC_UNROLL)
                        def _(j):
                            sl = pl.ds(j * SC_LANES, SC_LANES)
                            o = out_v[t, sl]
                            for r in range(SC_CHUNK):
                                o = o + splat[r] * bufs[b][r, sl]
                            out_v[t, sl] = o

                        finish_slot(item, b)

            pltpu.sync_copy(out_v, out_hbm.at[pl.ds(tok0, tb)])

    return sc_kernel(h, gate, eid, table_u, table_v)


def _final_kernel(x2_ref, o_ref, g_ref, y_ref):
    y_ref[...] = _rms(x2_ref[...] + o_ref[...], g_ref[...])


def _final(x2, out, g, tm):
    n, d = x2.shape
    spec = pl.BlockSpec((tm, d), lambda i: (i, 0))
    return pl.pallas_call(
        _final_kernel, grid=(n // tm,),
        in_specs=[spec, spec, pl.BlockSpec((1, d), lambda i: (0, 0))], out_specs=spec,
        out_shape=jax.ShapeDtypeStruct((n, d), F32),
        compiler_params=_cparams("parallel"), name="final_norm",
    )(x2, out, g)


def _peer_ffn_final(x2, h, eid, gate, peer_u, peer_v, g_final):
    n = x2.shape[0]
    per_w = n // (SC_CORES * SC_SUBCORES)
    tb = SC_TOKENS if per_w % SC_TOKENS == 0 else per_w
    tm = _row_tile(n, 512)
    out = _sc_peer(h, gate, eid, peer_u, peer_v, tb)
    return _final(x2, out, g_final, tm)


SC_LAG = 3


def _prompt_units(batch, seq):
    half = seq // 2
    return [(b, r0, half) for b in range(batch) for r0 in (0, half)]


def _rope_tables(first_pos, t):
    half = HEAD_DIM // 2
    inv_freq = ROPE_THETA ** (-jnp.arange(half, dtype=F32) / half)
    pos = first_pos + jnp.arange(t, dtype=I32)
    ang = pos.astype(F32)[:, None] * inv_freq[None, :]
    cos, sin = jnp.cos(ang), jnp.sin(ang)
    cos_t = jnp.concatenate([cos, cos, cos, cos], axis=1)
    sin_t = jnp.concatenate([-sin, sin, -sin, sin], axis=1)
    return cos_t, sin_t


def _sample_group(xs, g_mix, wbig, wsm, past, page_table, cache_k, cache_v, cache_idx_k, state_pool,
                  mem_k, mem_v, pw, ps, wo, g_cross, wq, wmo, g_ffn, wpq, sk, before_attn, before_tail):
    n_s = xs.shape[0]
    n_mem = mem_k.shape[1]
    cos_s, sin_s = _rope_tables(past, 1)
    cos_s = jnp.broadcast_to(cos_s, (n_s, LANES))
    sin_s = jnp.broadcast_to(sin_s, (n_s, LANES))
    q, kf, kb, vf, vb, qi, kif, kib, wi, p = _inproj(xs, g_mix, wbig, wsm, cos_s, sin_s, n_s)
    ck = jnp.transpose(cache_k, (0, 2, 3, 1))
    cv = jnp.transpose(cache_v, (0, 2, 3, 1))
    cik = jnp.transpose(cache_idx_k, (0, 2, 1))
    qi_s = jnp.transpose(qi, (1, 0, 2))
    scores, score_new = _idx_sample(page_table, qi_s, wi.reshape(n_s, N_IDX_HEADS, 1),
                                    kib.reshape(n_s, 1, IDX_HEAD_DIM), cik)
    k_top = min(TOPK_MAX, (past + 1) // 4)
    bias = _bias_sample(scores.reshape(n_s, past), score_new.reshape(n_s, LANES), k_top)
    heads = (n_s, N_HEADS, HEAD_DIM)
    bias, _ = lax.optimization_barrier((bias, before_attn))
    att = _attn_sample(page_table, q.reshape(heads), kf.reshape(heads), vf.reshape(heads),
                       bias.reshape(n_s, 1, past + LANES), ck, cv).reshape(n_s, ATT_WIDTH)
    att, _ = lax.optimization_barrier((att, before_tail))
    state_t = jnp.transpose(state_pool, (1, 0, 2))
    pool_out = _pool_sample(state_t, p, pw, ps)
    x1, qm = _outproj(xs, att, pool_out, wo, g_cross, wq, n_s)
    o = _cross_sample(qm, mem_k.reshape(n_s, n_mem, MEM_WIDTH),
                      mem_v.reshape(n_s, n_mem, MEM_WIDTH)).reshape(n_s, MEM_WIDTH)
    x2, h, eid, gate = _route(x1, o, wmo, g_ffn, wpq, sk, n_s)
    return x2, h, eid, gate, kf, vf, kif, p


def _row_tile(n, pref):
    return pref if n % pref == 0 else n


def kernel(x_prompt, x_sample, mem_prompt, cache_k, cache_v, cache_idx_k, state_pool, cache_mem_k,
           cache_mem_v, page_table, norm_mix, w_in, pool_w, pool_scale, w_out, norm_cross, norm_mem,
           w_mq, w_mk, w_mv, w_mo, norm_ffn, w_pq, sub_keys, peer_u, peer_v, norm_final):
    batch, seq, d = x_prompt.shape
    db, ds, _ = x_sample.shape
    depth = w_in.shape[0]
    assert depth == 1 and ds == 1
    n_pages = page_table.shape[1]
    past = n_pages * PAGE_SIZE
    n_mem = mem_prompt.shape[1]
    n_p, n_s = batch * seq, db * ds
    l = 0

    w = w_in[l]
    c_ki = 4 * ATT_WIDTH
    c_wi = c_ki + IDX_HEAD_DIM
    c_p = c_wi + N_IDX_HEADS
    wbig = jnp.concatenate([w[:, :c_ki], w[:, c_p:]], axis=1).astype(BF16)
    wsm = jnp.pad(w[:, c_ki:c_p], ((0, 0), (0, LANES - (c_p - c_ki)))).astype(BF16)
    g_mix = norm_mix[l][None, :]
    pw = pool_w[l].astype(BF16)
    ps = pool_scale[l][None, :]
    wo = w_out[l].astype(BF16)
    g_cross = norm_cross[l][None, :]
    wq = w_mq[l].astype(BF16)
    wmo = w_mo[l].astype(BF16)
    g_ffn = norm_ffn[l][None, :]
    wpq = w_pq[l].astype(BF16)
    sk = sub_keys[l].reshape(2 * PEER_HEADS, PEER_N_KEYS, PEER_HALF_DIM).astype(BF16)
    g_final = norm_final[None, :]
    pu, pv = peer_u[l], peer_v[l]

    tm = _row_tile(seq, 512)
    cos_p, sin_p = _rope_tables(0, seq)
    xp = x_prompt.reshape(n_p, d)
    q, kf, kb, vf, vb, qi, kif, kib, wi, p = _inproj(xp, g_mix, wbig, wsm, cos_p, sin_p, tm)
    pool_out = _pool_prompt(p, pw, ps, batch, tm)
    mkf, mvf, mkb, mvb = _memkv(mem_prompt.reshape(batch * n_mem, d), norm_mem[l][None, :],
                                w_mk[l].astype(BF16), w_mv[l].astype(BF16), _row_tile(batch * n_mem, 256))
    ys = []
    xs = x_sample.reshape(n_s, d)
    wi_p = wi
    for b, r0, rows in _prompt_units(batch, seq):
        if len(ys) >= SC_LAG:
            ys[-SC_LAG], wi_p = lax.optimization_barrier((ys[-SC_LAG], wi_p))
        att = _dsa_prompt(q, qi, wi_p, kb, vb, kib, b, r0, rows, seq)
        x1, qm = _outproj(xp, att, pool_out, wo, g_cross, wq, tm, row0=b * seq + r0)
        o = _cross_prompt(qm, mkb, mvb, b, 1, n_mem, tm)
        x2, h, eid, gate = _route(x1, o, wmo, g_ffn, wpq, sk, _row_tile(rows, 256))
        eid, wi_p, xs = lax.optimization_barrier((eid, wi_p, xs))
        ys.append(_peer_ffn_final(x2, h, eid, gate, pu, pv, g_final))
    y_prompt = jnp.concatenate(ys, axis=0).reshape(batch, seq, d)

    new_k_prompt = kf.reshape(1, batch, seq, N_HEADS, HEAD_DIM)
    new_v_prompt = vf.reshape(1, batch, seq, N_HEADS, HEAD_DIM)
    new_idx_k_prompt = kif.reshape(1, batch, seq, IDX_HEAD_DIM)
    new_pool_prompt = p.reshape(batch, seq, POOL_WIDTH)[None, :, seq - POOL_STATE_LEN:, :]
    new_mem_k_prompt = mkf.reshape(1, batch, n_mem, N_MEM_HEADS, MEM_HEAD_DIM)
    new_mem_v_prompt = mvf.reshape(1, batch, n_mem, N_MEM_HEADS, MEM_HEAD_DIM)

    if len(ys) > 2:
        ys[-3], xs = lax.optimization_barrier((ys[-3], xs))
    x2, h, eid, gate, kf, vf, kif, p = _sample_group(
        xs, g_mix, wbig, wsm, past, page_table, cache_k[l], cache_v[l], cache_idx_k[l], state_pool[l],
        cache_mem_k[l], cache_mem_v[l], pw, ps, wo, g_cross, wq, wmo, g_ffn, wpq, sk,
        before_attn=ys[-2], before_tail=ys[-1])
    y_sample = _peer_ffn_final(x2, h, eid, gate, pu, pv, g_final).reshape(db, ds, d)

    new_k_sample = kf.reshape(1, db, ds, N_HEADS, HEAD_DIM)
    new_v_sample = vf.reshape(1, db, ds, N_HEADS, HEAD_DIM)
    new_idx_k_sample = kif.reshape(1, db, ds, IDX_HEAD_DIM)
    new_pool_sample = jnp.concatenate([state_pool[l][:, 1:, :], p[:, None, :]], axis=1)[None]

    return (y_prompt, y_sample, new_k_prompt, new_v_prompt, new_idx_k_prompt, new_pool_prompt,
            new_mem_k_prompt, new_mem_v_prompt, new_k_sample, new_v_sample, new_idx_k_sample,
            new_pool_sample)
```
